```python
import math
import functools
import jax
import jax.numpy as jnp
from jax import lax
import numpy as np

D_MODEL = 1024
BATCH = 2
SEQ = 8192
DEPTH = 1
DEC_BATCH = 128
DEC_SEQ = 4
PAST_LEN = 8192
PAGE_SIZE = 128

HEAD_DIM = 128
N_DELTA_HEADS = 4
N_MOBA_HEADS = 4
DELTA_WIDTH = N_DELTA_HEADS * HEAD_DIM
MOBA_WIDTH = N_MOBA_HEADS * HEAD_DIM
MIX_WIDTH = DELTA_WIDTH + MOBA_WIDTH
CONV_WIDTH = 4
CONV_CH = 3 * DELTA_WIDTH
DELTA_CHUNK = 64
MOBA_BLOCK = 256
MOBA_TOPK = 3
Q_BLOCK = 128
ROPE_THETA = 10000.0
D_FF = ((8 * D_MODEL + 3 * 256 - 1) // (3 * 256)) * 256
PLE_DIM = 256
RMS_EPS = 1e-6
L2_EPS = 1e-6
IN_SPLITS = [CONV_CH, CONV_CH + DELTA_WIDTH, CONV_CH + DELTA_WIDTH + N_DELTA_HEADS, CONV_CH + DELTA_WIDTH + 2 * N_DELTA_HEADS]
IN_WIDTH = CONV_CH + DELTA_WIDTH + 2 * N_DELTA_HEADS + 3 * MOBA_WIDTH

kernel_name = 'hymba_gdn_moba_step'


def _rmsnorm(x, g):
    xf = x.astype(jnp.float32)
    y = xf * lax.rsqrt(jnp.mean(xf * xf, axis=-1, keepdims=True) + RMS_EPS)
    return (y * g.astype(jnp.float32)).astype(x.dtype)


def _l2norm(x):
    xf = x.astype(jnp.float32)
    return xf * lax.rsqrt(jnp.sum(xf * xf, axis=-1, keepdims=True) + L2_EPS)


def _rope(x, pos):
    half = HEAD_DIM // 2
    inv_freq = jnp.exp(jnp.arange(half, dtype=jnp.float32) * (-2.0 * math.log(ROPE_THETA) / HEAD_DIM))
    ang = pos.astype(jnp.float32)[:, None] * inv_freq[None, :]
    cos = jnp.cos(ang)[:, None, :]
    sin = jnp.sin(ang)[:, None, :]
    xf = x.astype(jnp.float32)
    x1, x2 = xf[..., :half], xf[..., half:]
    return jnp.concatenate([x1 * cos - x2 * sin, x2 * cos + x1 * sin], axis=-1).astype(x.dtype)


def _short_conv(x, hist, w):
    xx = jnp.concatenate([hist.astype(x.dtype), x], axis=1)
    y = lax.conv_general_dilated(xx, w[:, None, :].astype(x.dtype), window_strides=(1,), padding='VALID',
                                 dimension_numbers=('NWC', 'WIO', 'NWC'), feature_group_count=CONV_CH)
    return jax.nn.silu(y), xx[:, xx.shape[1] - (CONV_WIDTH - 1):]


def _gated_delta_rule(q, k, v, g, beta, s0):
    bn, length, nh, dk = q.shape
    dv = v.shape[-1]
    c = min(DELTA_CHUNK, length)
    n = -(-length // c)
    pad = n * c - length
    if pad:
        pw = ((0, 0), (0, pad), (0, 0))
        q = jnp.pad(q, pw + ((0, 0),))
        k = jnp.pad(k, pw + ((0, 0),))
        v = jnp.pad(v, pw + ((0, 0),))
        g = jnp.pad(g, pw)
        beta = jnp.pad(beta, pw)

    def chunks(t):
        t = jnp.moveaxis(t, 2, 1)
        return t.reshape((bn, nh, n, c) + t.shape[3:])

    q = chunks(q) * (dk ** -0.5)
    k = chunks(k)
    v = chunks(v)
    g = chunks(g)
    beta = chunks(beta)
    gc = jnp.cumsum(g, axis=-1)
    causal = jnp.tril(jnp.ones((c, c), dtype=bool))
    strict = jnp.tril(jnp.ones((c, c), dtype=bool), -1)
    decay = jnp.exp(jnp.where(causal, gc[..., :, None] - gc[..., None, :], -jnp.inf))
    kb = k * beta[..., None]
    a_mat = jnp.where(strict, jnp.einsum('bhnid,bhnjd->bhnij', kb, k) * decay, 0.0)
    t_mat = a_mat + jnp.eye(c, dtype=a_mat.dtype)
    rhs = jnp.concatenate([v * beta[..., None], kb * jnp.exp(gc)[..., None]], axis=-1)
    sol = lax.linalg.triangular_solve(t_mat, rhs, left_side=True, lower=True, unit_diagonal=True)
    u, wk = sol[..., :dv], sol[..., dv:]
    qk = jnp.where(causal, jnp.einsum('bhnid,bhnjd->bhnij', q, k) * decay, 0.0)
    q_dec = q * jnp.exp(gc)[..., None]
    k_dec = k * jnp.exp(gc[..., -1:] - gc)[..., None]
    c_dec = jnp.exp(gc[..., -1])

    def step(s, xs):
        u_c, wk_c, qk_c, qd_c, kd_c, cd_c = xs
        w_c = u_c - jnp.einsum('bhid,bhde->bhie', wk_c, s)
        o_c = jnp.einsum('bhid,bhde->bhie', qd_c, s) + jnp.einsum('bhij,bhje->bhie', qk_c, w_c)
        s = s * cd_c[..., None, None] + jnp.einsum('bhid,bhie->bhde', kd_c, w_c)
        return s, o_c

    xs = tuple(jnp.moveaxis(t, 2, 0) for t in (u, wk, qk, q_dec, k_dec, c_dec))
    s_fin, o = lax.scan(step, s0, xs)
    o = jnp.moveaxis(o, 0, 2).reshape(bn, nh, n * c, dv)
    return jnp.moveaxis(o, 1, 2)[:, :length], s_fin


def _delta_mixer(qkv, z, b_logit, a_logit, conv_hist, s0, conv_w, a_log, dt_bias, delta_norm):
    bn, length, _ = qkv.shape
    qkv_c, conv_new = _short_conv(qkv, conv_hist, conv_w)
    q, k, v = jnp.split(qkv_c, 3, axis=-1)
    shp = (bn, length, N_DELTA_HEADS, HEAD_DIM)
    q = _l2norm(q.reshape(shp))
    k = _l2norm(k.reshape(shp))
    v = v.reshape(shp).astype(jnp.float32)
    beta = jax.nn.sigmoid(b_logit.astype(jnp.float32))
    g = -jnp.exp(a_log.astype(jnp.float32)) * jax.nn.softplus(a_logit.astype(jnp.float32) + dt_bias.astype(jnp.float32))
    o, s_new = _gated_delta_rule(q, k, v, g, beta, s0.astype(jnp.float32))
    o = _rmsnorm(o, delta_norm) * jax.nn.silu(z.reshape(shp).astype(jnp.float32))
    return o.reshape(bn, length, DELTA_WIDTH).astype(qkv.dtype), conv_new, s_new


def _moba_prompt(q, k, v):
    bn, s_len, nh, d = q.shape
    nb = -(-s_len // MOBA_BLOCK)
    pad = nb * MOBA_BLOCK - s_len
    kp = jnp.pad(k, ((0, 0), (0, pad), (0, 0), (0, 0)))
    vp = jnp.pad(v, ((0, 0), (0, pad), (0, 0), (0, 0)))
    kb = kp.reshape(bn, nb, MOBA_BLOCK, nh, d)
    vb = vp.reshape(bn, nb, MOBA_BLOCK, nh, d)
    kmean = jnp.mean(kb.astype(jnp.float32), axis=2)
    n_sel = min(MOBA_TOPK, nb - 1)
    scale = HEAD_DIM ** -0.5
    b_i = jnp.arange(bn)[:, None, None, None]
    h_i = jnp.arange(nh)[None, :, None, None]

    def attend_block(qb):
        q_blk = lax.dynamic_slice_in_dim(q, qb * Q_BLOCK, Q_BLOCK, axis=1).astype(jnp.float32)
        cur = qb // (MOBA_BLOCK // Q_BLOCK)
        q_pos = qb * Q_BLOCK + jnp.arange(Q_BLOCK)
        k_own = lax.dynamic_index_in_dim(kb, cur, axis=1, keepdims=False).astype(jnp.float32)
        v_own = lax.dynamic_index_in_dim(vb, cur, axis=1, keepdims=False).astype(jnp.float32)
        k_pos = cur * MOBA_BLOCK + jnp.arange(MOBA_BLOCK)
        s_own = jnp.einsum('bqhd,bkhd->bhqk', q_blk, k_own) * scale
        s_own = jnp.where(k_pos[None, :] <= q_pos[:, None], s_own, -jnp.inf)
        if n_sel > 0:
            gate = jnp.einsum('bqhd,bnhd->bhqn', q_blk, kmean)
            gate = jnp.where(jnp.arange(nb) < cur, gate, -jnp.inf)
            _, idx = lax.top_k(gate, n_sel)
            k_sel = kb[b_i, idx, :, h_i].astype(jnp.float32)
            v_sel = vb[b_i, idx, :, h_i].astype(jnp.float32)
            s_sel = jnp.einsum('bqhd,bhqnkd->bhqnk', q_blk, k_sel) * scale
            s_sel = jnp.where((idx < cur)[..., None], s_sel, -jnp.inf).reshape(bn, nh, Q_BLOCK, n_sel * MOBA_BLOCK)
            p = jax.nn.softmax(jnp.concatenate([s_sel, s_own], axis=-1), axis=-1)
            p_sel = p[..., :n_sel * MOBA_BLOCK].reshape(bn, nh, Q_BLOCK, n_sel, MOBA_BLOCK)
            o = (jnp.einsum('bhqnk,bhqnkd->bqhd', p_sel, v_sel)
                 + jnp.einsum('bhqk,bkhd->bqhd', p[..., n_sel * MOBA_BLOCK:], v_own))
        else:
            o = jnp.einsum('bhqk,bkhd->bqhd', jax.nn.softmax(s_own, axis=-1), v_own)
        return o.astype(q.dtype)

    o = lax.map(attend_block, jnp.arange(s_len // Q_BLOCK))
    return jnp.moveaxis(o, 0, 1).reshape(bn, s_len, nh, d)


def _moba_sample(q, k_new, v_new, k_pool, v_pool, page_table):
    bn, length, nh, d = q.shape
    n_pages = page_table.shape[1]
    ppb = MOBA_BLOCK // PAGE_SIZE
    n_full = (n_pages * PAGE_SIZE) // MOBA_BLOCK
    own_first = n_full * ppb
    n_own_past = (n_pages - own_first) * PAGE_SIZE
    n_sel = min(MOBA_TOPK, n_full)
    scale = HEAD_DIM ** -0.5
    qf = q.astype(jnp.float32)
    own_phys = page_table[:, own_first:]
    k_own = jnp.concatenate([k_pool[own_phys].reshape(bn, n_own_past, nh, d), k_new], axis=1).astype(jnp.float32)
    v_own = jnp.concatenate([v_pool[own_phys].reshape(bn, n_own_past, nh, d), v_new], axis=1).astype(jnp.float32)
    s_own = jnp.einsum('bqhd,bkhd->bhqk', qf, k_own) * scale
    own_mask = jnp.arange(n_own_past + length)[None, :] <= (n_own_past + jnp.arange(length))[:, None]
    s_own = jnp.where(own_mask, s_own, -jnp.inf)
    if n_sel > 0:
        page_sum = jnp.sum(k_pool.astype(jnp.float32), axis=1)
        blk_sum = page_sum[page_table[:, :own_first]].reshape(bn, n_full, ppb, nh, d).sum(axis=2)
        kmean = blk_sum / MOBA_BLOCK
        gate = jnp.einsum('bqhd,bnhd->bhqn', qf, kmean)
        _, idx = lax.top_k(gate, n_sel)
        logical = idx[..., None] * ppb + jnp.arange(ppb)
        phys = page_table[jnp.arange(bn)[:, None, None, None, None], logical]
        h_i = jnp.arange(nh)[None, :, None, None, None]
        k_sel = k_pool[phys, :, h_i].reshape(bn, nh, length, n_sel * MOBA_BLOCK, d).astype(jnp.float32)
        v_sel = v_pool[phys, :, h_i].reshape(bn, nh, length, n_sel * MOBA_BLOCK, d).astype(jnp.float32)
        s_sel = jnp.einsum('bqhd,bhqkd->bhqk', qf, k_sel) * scale
        p = jax.nn.softmax(jnp.concatenate([s_sel, s_own], axis=-1), axis=-1)
        o = (jnp.einsum('bhqk,bhqkd->bqhd', p[..., :n_sel * MOBA_BLOCK], v_sel)
             + jnp.einsum('bhqk,bkhd->bqhd', p[..., n_sel * MOBA_BLOCK:], v_own))
    else:
        o = jnp.einsum('bhqk,bkhd->bqhd', jax.nn.softmax(s_own, axis=-1), v_own)
    return o.astype(q.dtype)


def _layer(h, pos, ple, conv_hist, s0, attend, attn_norm, w_in, conv_w, a_log, dt_bias, delta_norm,
           w_out, ffn_norm, w_gate, w_up, w_down, w_ple, w_ple_gate):
    bn, length, _ = h.shape
    a = _rmsnorm(h, attn_norm)
    proj = a @ w_in
    qkv_d, z, b_logit, a_logit, qkv_m = jnp.split(proj, IN_SPLITS, axis=-1)
    o_delta, conv_new, s_new = _delta_mixer(qkv_d, z, b_logit, a_logit, conv_hist, s0, conv_w, a_log, dt_bias, delta_norm)
    hs = (bn, length, N_MOBA_HEADS, HEAD_DIM)
    q_m, k_m, v_m = jnp.split(qkv_m, 3, axis=-1)
    q_m = _rope(q_m.reshape(hs), pos)
    k_m = _rope(k_m.reshape(hs), pos)
    v_m = v_m.reshape(hs)
    o_moba = attend(q_m, k_m, v_m).reshape(bn, length, MOBA_WIDTH)
    h = h + jnp.concatenate([o_delta.astype(h.dtype), o_moba.astype(h.dtype)], axis=-1) @ w_out
    f = _rmsnorm(h, ffn_norm)
    h = h + (jax.nn.silu(f @ w_gate) * (f @ w_up)) @ w_down
    h = h + (ple @ w_ple) * jax.nn.sigmoid(h @ w_ple_gate)
    return h, k_m, v_m, conv_new, s_new


def setup_inputs(seed: int = 0) -> dict:
    key = jax.random.key(seed)
    ks = jax.random.split(key, 24)
    f32 = jnp.float32
    n_pages = PAST_LEN // PAGE_SIZE
    n_pool = (DEC_BATCH * n_pages * 5) // 4

    def nrm(k, shape, s):
        return jax.random.normal(k, shape, f32) * s

    x_prompt = nrm(ks[0], (BATCH, SEQ, D_MODEL), 1.0)
    x_sample = nrm(ks[1], (DEC_BATCH, DEC_SEQ, D_MODEL), 1.0)
    cache_k = nrm(ks[2], (DEPTH, n_pool, PAGE_SIZE, N_MOBA_HEADS, HEAD_DIM), 1.0)
    cache_v = nrm(ks[3], (DEPTH, n_pool, PAGE_SIZE, N_MOBA_HEADS, HEAD_DIM), 1.0)
    page_table = jax.random.permutation(ks[4], n_pool)[:DEC_BATCH * n_pages].reshape(DEC_BATCH, n_pages).astype(jnp.int32)
    state_delta = nrm(ks[5], (DEPTH, DEC_BATCH, N_DELTA_HEADS, HEAD_DIM, HEAD_DIM), 0.5)
    state_conv = nrm(ks[6], (DEPTH, DEC_BATCH, CONV_WIDTH - 1, CONV_CH), 1.0)
    p_prompt = nrm(ks[7], (DEPTH, BATCH, SEQ, PLE_DIM), 1.0)
    p_sample = nrm(ks[8], (DEPTH, DEC_BATCH, DEC_SEQ, PLE_DIM), 1.0)
    attn_norm = 1.0 + nrm(ks[9], (DEPTH, D_MODEL), 0.02)
    w_in = nrm(ks[10], (DEPTH, D_MODEL, IN_WIDTH), D_MODEL ** -0.5)
    conv_w = nrm(ks[11], (DEPTH, CONV_WIDTH, CONV_CH), CONV_WIDTH ** -0.5)
    a_log = jnp.log(jax.random.uniform(ks[12], (DEPTH, N_DELTA_HEADS), f32, 1.0, 16.0))
    dt = jnp.exp(jax.random.uniform(ks[13], (DEPTH, N_DELTA_HEADS), f32, math.log(1e-3), math.log(1e-1)))
    dt_bias = dt + jnp.log(-jnp.expm1(-dt))
    delta_norm = 1.0 + nrm(ks[14], (DEPTH, HEAD_DIM), 0.02)
    w_out = nrm(ks[15], (DEPTH, MIX_WIDTH, D_MODEL), MIX_WIDTH ** -0.5)
    ffn_norm = 1.0 + nrm(ks[16], (DEPTH, D_MODEL), 0.02)
    w_gate = nrm(ks[17], (DEPTH, D_MODEL, D_FF), D_MODEL ** -0.5)
    w_up = nrm(ks[18], (DEPTH, D_MODEL, D_FF), D_MODEL ** -0.5)
    w_down = nrm(ks[19], (DEPTH, D_FF, D_MODEL), D_FF ** -0.5)
    w_ple = nrm(ks[20], (DEPTH, PLE_DIM, D_MODEL), PLE_DIM ** -0.5)
    w_ple_gate = nrm(ks[21], (DEPTH, D_MODEL, D_MODEL), D_MODEL ** -0.5)
    final_norm = 1.0 + nrm(ks[22], (D_MODEL,), 0.02)
    return {'x_prompt': x_prompt, 'x_sample': x_sample, 'cache_k': cache_k, 'cache_v': cache_v,
            'page_table': page_table, 'state_delta': state_delta, 'state_conv': state_conv,
            'p_prompt': p_prompt, 'p_sample': p_sample, 'attn_norm': attn_norm, 'w_in': w_in,
            'conv_w': conv_w, 'a_log': a_log, 'dt_bias': dt_bias, 'delta_norm': delta_norm, 'w_out': w_out,
            'ffn_norm': ffn_norm, 'w_gate': w_gate, 'w_up': w_up, 'w_down': w_down, 'w_ple': w_ple,
            'w_ple_gate': w_ple_gate, 'final_norm': final_norm}


def reference(x_prompt, x_sample, cache_k, cache_v, page_table, state_delta, state_conv, p_prompt, p_sample,
              attn_norm, w_in, conv_w, a_log, dt_bias, delta_norm, w_out, ffn_norm, w_gate, w_up, w_down,
              w_ple, w_ple_gate, final_norm):
    bp, s_len, _ = x_prompt.shape
    bs, length, _ = x_sample.shape
    past = page_table.shape[1] * PAGE_SIZE
    pos_prompt = jnp.arange(s_len, dtype=jnp.int32)
    pos_sample = past + jnp.arange(length, dtype=jnp.int32)
    hp, hs = x_prompt, x_sample
    kp_l, vp_l, ks_l, vs_l, dp_l, ds_l, cp_l, cs_l = [], [], [], [], [], [], [], []
    for i in range(DEPTH):
        conv0 = jnp.zeros((bp, CONV_WIDTH - 1, CONV_CH), x_prompt.dtype)
        s00 = jnp.zeros((bp, N_DELTA_HEADS, HEAD_DIM, HEAD_DIM), jnp.float32)
        hp, k_p, v_p, c_p, d_p = _layer(hp, pos_prompt, p_prompt[i], conv0, s00, _moba_prompt,
                                        attn_norm[i], w_in[i], conv_w[i], a_log[i], dt_bias[i], delta_norm[i],
                                        w_out[i], ffn_norm[i], w_gate[i], w_up[i], w_down[i], w_ple[i], w_ple_gate[i])
        attend_s = functools.partial(_moba_sample, k_pool=cache_k[i], v_pool=cache_v[i], page_table=page_table)
        hs, k_s, v_s, c_s, d_s = _layer(hs, pos_sample, p_sample[i], state_conv[i], state_delta[i], attend_s,
                                        attn_norm[i], w_in[i], conv_w[i], a_log[i], dt_bias[i], delta_norm[i],
                                        w_out[i], ffn_norm[i], w_gate[i], w_up[i], w_down[i], w_ple[i], w_ple_gate[i])
        kp_l.append(k_p)
        vp_l.append(v_p)
        ks_l.append(k_s)
        vs_l.append(v_s)
        dp_l.append(d_p.astype(state_delta.dtype))
        ds_l.append(d_s.astype(state_delta.dtype))
        cp_l.append(c_p.astype(state_conv.dtype))
        cs_l.append(c_s.astype(state_conv.dtype))
    y_prompt = _rmsnorm(hp, final_norm)
    y_sample = _rmsnorm(hs, final_norm)
    return (y_prompt, y_sample, jnp.stack(kp_l), jnp.stack(vp_l), jnp.stack(ks_l), jnp.stack(vs_l),
            jnp.stack(dp_l), jnp.stack(ds_l), jnp.stack(cp_l), jnp.stack(cs_l))
```

```python
import functools
import math

import jax
import jax.numpy as jnp
from jax import lax
from jax.experimental import pallas as pl
from jax.experimental.pallas import tpu as pltpu

F32 = jnp.float32
BF16 = jnp.bfloat16
HI = lax.Precision.HIGHEST

HEAD_DIM = 128
N_HEADS = 4
GROUP_WIDTH = N_HEADS * HEAD_DIM
CONV_WIDTH = 4
CONV_CH = 3 * GROUP_WIDTH
DELTA_CHUNK = 64
MOBA_BLOCK = 256
MOBA_TOPK = 3
ROPE_THETA = 10000.0
RMS_EPS = 1e-6
L2_EPS = 1e-6
NEG = -1e30
LANE = 128
SUBLANE = 8
VMEM_LIMIT = 56 * 1024 * 1024

C_QKV = 0
C_Z = CONV_CH
C_QM = C_Z + GROUP_WIDTH
C_KM = C_QM + GROUP_WIDTH
C_VM = C_KM + GROUP_WIDTH
C_GATE = C_VM + GROUP_WIDTH
IN_COLS = C_GATE + LANE


def _cparams(sem):
    return pltpu.CompilerParams(dimension_semantics=sem, vmem_limit_bytes=VMEM_LIMIT)


def _resident(shape):
    nd = len(shape)
    return pl.BlockSpec(shape, lambda *_: (0,) * nd, pipeline_mode=pl.Buffered(1))


def _sigmoid(x):
    return 1.0 / (1.0 + jnp.exp(-x))


def _silu(x):
    return x * _sigmoid(x)


def _softplus(x):
    return jnp.maximum(x, 0.0) + jnp.log(1.0 + jnp.exp(-jnp.abs(x)))


def _rms(x, g):
    return x * lax.rsqrt(jnp.mean(x * x, axis=-1, keepdims=True) + RMS_EPS) * g


def _dot(a, b, prec=None):
    return jnp.dot(a, b, precision=prec, preferred_element_type=F32)


def _dot_nt(a, b, prec=None):
    return lax.dot_general(a, b, (((1,), (1,)), ((), ())), precision=prec, preferred_element_type=F32)


def _bmm(a, b, prec=None):
    return lax.dot_general(a, b, (((2,), (1,)), ((0,), (0,))), precision=prec, preferred_element_type=F32)


def _bmm_nt(a, b, prec=None):
    return lax.dot_general(a, b, (((2,), (2,)), ((0,), (0,))), precision=prec, preferred_element_type=F32)


def _bmm_tn(a, b, prec=None):
    return lax.dot_general(a, b, (((1,), (1,)), ((0,), (0,))), precision=prec, preferred_element_type=F32)


def _in_proj_kernel(x_ref, g_ref, w_ref, cos_ref, sin_ref,
                    qkv_ref, z_ref, q_ref, k_ref, v_ref, gate_ref, kbf_ref, vbf_ref, ksum_ref):
    a = _rms(x_ref[...], g_ref[...]).astype(BF16)
    cos = cos_ref[...]
    sin = sin_ref[...]

    def proj(c0, width):
        return _dot(a, w_ref[:, c0:c0 + width])

    def rope(t):
        heads = []
        for h in range(N_HEADS):
            th = t[:, h * HEAD_DIM:(h + 1) * HEAD_DIM]
            heads.append(th * cos + pltpu.roll(th, HEAD_DIM // 2, axis=1) * sin)
        return jnp.concatenate(heads, axis=1)

    qkv_ref[...] = proj(C_QKV, CONV_CH)
    z_ref[...] = proj(C_Z, GROUP_WIDTH)
    q_ref[...] = rope(proj(C_QM, GROUP_WIDTH))
    k = rope(proj(C_KM, GROUP_WIDTH))
    v = proj(C_VM, GROUP_WIDTH)
    k_ref[...] = k
    v_ref[...] = v
    kbf_ref[...] = k.astype(BF16)
    vbf_ref[...] = v.astype(BF16)
    nblk = ksum_ref.shape[1]
    ksum_ref[0] = jnp.sum(k.reshape(nblk, k.shape[0] // nblk, GROUP_WIDTH), axis=1)
    gate_ref[...] = proj(C_GATE, LANE)


def _in_proj(x, norm_w, w_in_r, cos, sin, tm):
    t, d = x.shape
    nt = t // tm
    n_tab = cos.shape[0] // tm
    nblk = max(tm // MOBA_BLOCK, 1)
    tok = lambda w: pl.BlockSpec((tm, w), lambda i: (i, 0))
    tab = pl.BlockSpec((tm, HEAD_DIM), lambda i: (i % n_tab, 0))
    out_shapes = [
        jax.ShapeDtypeStruct((t, CONV_CH), F32), jax.ShapeDtypeStruct((t, GROUP_WIDTH), F32),
        jax.ShapeDtypeStruct((t, GROUP_WIDTH), F32), jax.ShapeDtypeStruct((t, GROUP_WIDTH), F32),
        jax.ShapeDtypeStruct((t, GROUP_WIDTH), F32), jax.ShapeDtypeStruct((t, LANE), F32),
        jax.ShapeDtypeStruct((t, GROUP_WIDTH), BF16), jax.ShapeDtypeStruct((t, GROUP_WIDTH), BF16),
        jax.ShapeDtypeStruct((nt, nblk, GROUP_WIDTH), F32),
    ]
    out_specs = [tok(CONV_CH), tok(GROUP_WIDTH), tok(GROUP_WIDTH), tok(GROUP_WIDTH), tok(GROUP_WIDTH),
                 tok(LANE), tok(GROUP_WIDTH), tok(GROUP_WIDTH),
                 pl.BlockSpec((1, nblk, GROUP_WIDTH), lambda i: (i, 0, 0))]
    return pl.pallas_call(
        _in_proj_kernel,
        grid=(nt,),
        in_specs=[tok(d), _resident((1, d)), _resident(w_in_r.shape), tab, tab],
        out_specs=out_specs,
        out_shape=out_shapes,
        compiler_params=_cparams(("arbitrary",)),
        name="in_proj",
    )(x, norm_w, w_in_r, cos, sin)


def _chunk_prep(q, k, v, g, beta):
    n, c, dh = q.shape
    row = lax.broadcasted_iota(jnp.int32, (c, c), 0)
    col = lax.broadcasted_iota(jnp.int32, (c, c), 1)
    causal = row >= col
    strict = row > col
    lower = jnp.broadcast_to(causal.astype(F32), (n, c, c))
    dlog = _bmm(lower, jnp.where(strict, jnp.broadcast_to(g, (n, c, c)), 0.0), HI)
    gcb = _bmm(lower, jnp.broadcast_to(g, (n, c, dh)), HI)
    decay = jnp.where(causal, jnp.exp(jnp.where(causal, dlog, 0.0)), 0.0)
    kb = k * beta
    a_mat = jnp.where(strict, _bmm_nt(kb, k, HI) * decay, 0.0)
    nmat = -a_mat
    t_inv = jnp.where(row == col, 1.0, 0.0) + nmat
    for _ in range(int(math.log2(c)) - 1):
        nmat = _bmm(nmat, nmat, HI)
        t_inv = t_inv + _bmm(t_inv, nmat, HI)
    egc = jnp.exp(gcb)
    u = _bmm(t_inv, v * beta, HI)
    wk = _bmm(t_inv, kb * egc, HI)
    qk = jnp.where(causal, _bmm_nt(q, k, HI) * decay, 0.0)
    g_last = gcb[:, c - 1:c, :]
    return u, wk, qk, q * egc, k * jnp.exp(g_last - gcb), jnp.exp(g_last)


def _chunk_apply(s, u, wk, qk, qd, kd, cd):
    c = u.shape[1]
    xs = _bmm(jnp.concatenate([wk, qd], axis=1).astype(BF16), s.astype(BF16))
    w = u - xs[:, :c]
    wb = w.astype(BF16)
    o = xs[:, c:] + _bmm(qk.astype(BF16), wb)
    s_new = s * cd + _bmm_tn(kd.astype(BF16), wb)
    return o, s_new


def _delta_inputs(y, gates, alog, dtb, h):
    sl = lambda base: slice(base + h * HEAD_DIM, base + (h + 1) * HEAD_DIM)
    q = y[:, sl(0)]
    k = y[:, sl(GROUP_WIDTH)]
    v = y[:, sl(2 * GROUP_WIDTH)]
    q = q * lax.rsqrt(jnp.sum(q * q, axis=-1, keepdims=True) + L2_EPS) * (HEAD_DIM ** -0.5)
    k = k * lax.rsqrt(jnp.sum(k * k, axis=-1, keepdims=True) + L2_EPS)
    beta = _sigmoid(gates[:, h:h + 1])
    g = -jnp.exp(alog[:, N_HEADS + h:N_HEADS + h + 1]) * _softplus(
        gates[:, N_HEADS + h:N_HEADS + h + 1] + dtb[:, N_HEADS + h:N_HEADS + h + 1])
    return q, k, v, g, beta


def _delta_prep_kernel(qkv_ref, gate_ref, hist_ref, convw_ref, alog_ref, dtb_ref,
                       u_ref, wk_ref, qd_ref, kd_ref, qk_ref, cd_ref, prev_ref):
    tt = qkv_ref.shape[1]
    n = tt // DELTA_CHUNK

    @pl.when(pl.program_id(1) == 0)
    def _():
        prev_ref[...] = jnp.zeros_like(prev_ref)
        prev_ref[SUBLANE - (CONV_WIDTH - 1):, :] = hist_ref[0]

    x = qkv_ref[0]
    full = jnp.concatenate([prev_ref[...], x], axis=0)
    w = convw_ref[...]
    base = SUBLANE - (CONV_WIDTH - 1)
    y = full[base:base + tt] * w[0:1]
    for j in range(1, CONV_WIDTH):
        y = y + full[base + j:base + j + tt] * w[j:j + 1]
    y = _silu(y)
    prev_ref[...] = x[tt - SUBLANE:]

    gates = gate_ref[0]
    alog = alog_ref[...]
    dtb = dtb_ref[...]
    for h in range(N_HEADS):
        q, k, v, g, beta = _delta_inputs(y, gates, alog, dtb, h)
        r3 = lambda t: t.reshape(n, DELTA_CHUNK, t.shape[-1])
        u, wk, qk, qd, kd, cd = _chunk_prep(r3(q), r3(k), r3(v), r3(g), r3(beta))
        hs = slice(h * HEAD_DIM, (h + 1) * HEAD_DIM)
        u_ref[0, :, hs] = u.reshape(tt, HEAD_DIM)
        wk_ref[0, :, hs] = wk.reshape(tt, HEAD_DIM)
        qd_ref[0, :, hs] = qd.reshape(tt, HEAD_DIM)
        kd_ref[0, :, hs] = kd.reshape(tt, HEAD_DIM)
        qk_ref[0, :, h * DELTA_CHUNK:(h + 1) * DELTA_CHUNK] = qk.reshape(tt, DELTA_CHUNK)
        cd_ref[0, :, hs] = cd.reshape(n, HEAD_DIM)


def _delta_prep(qkv, gates, hist, conv_w, alog_row, dtb_row, tt):
    b, s, _ = qkv.shape
    n = tt // DELTA_CHUNK
    tok = lambda w: pl.BlockSpec((1, tt, w), lambda bi, j: (bi, j, 0))
    wide = jax.ShapeDtypeStruct((b, s, GROUP_WIDTH), F32)
    return pl.pallas_call(
        _delta_prep_kernel,
        grid=(b, s // tt),
        in_specs=[tok(CONV_CH), tok(LANE),
                  pl.BlockSpec((1, CONV_WIDTH - 1, CONV_CH), lambda bi, j: (bi, 0, 0)),
                  _resident(conv_w.shape), _resident((1, LANE)), _resident((1, LANE))],
        out_specs=[tok(GROUP_WIDTH), tok(GROUP_WIDTH), tok(GROUP_WIDTH), tok(GROUP_WIDTH),
                   tok(N_HEADS * DELTA_CHUNK),
                   pl.BlockSpec((1, n, GROUP_WIDTH), lambda bi, j: (bi, j, 0))],
        out_shape=[wide, wide, wide, wide,
                   jax.ShapeDtypeStruct((b, s, N_HEADS * DELTA_CHUNK), F32),
                   jax.ShapeDtypeStruct((b, s // DELTA_CHUNK, GROUP_WIDTH), F32)],
        scratch_shapes=[pltpu.VMEM((SUBLANE, CONV_CH), F32)],
        compiler_params=_cparams(("arbitrary", "arbitrary")),
        name="delta_prep",
    )(qkv, gates, hist, conv_w, alog_row, dtb_row)


def _delta_scan_kernel(u_ref, wk_ref, qd_ref, kd_ref, qk_ref, cd_ref, z_ref, s0_ref, dnorm_ref,
                       o_ref, sfin_ref, s_scr):
    nb, tt, _ = u_ref.shape
    n = tt // DELTA_CHUNK

    @pl.when(pl.program_id(0) == 0)
    def _():
        s_scr[...] = s0_ref[...]

    dnorm = dnorm_ref[...]
    for ci in range(n):
        rows = slice(ci * DELTA_CHUNK, (ci + 1) * DELTA_CHUNK)
        for b in range(nb):
            for h in range(N_HEADS):
                hs = slice(h * HEAD_DIM, (h + 1) * HEAD_DIM)
                o, s_new = _chunk_apply(
                    s_scr[b, h][None], u_ref[b, rows, hs][None], wk_ref[b, rows, hs][None],
                    qk_ref[b, rows, h * DELTA_CHUNK:(h + 1) * DELTA_CHUNK][None],
                    qd_ref[b, rows, hs][None], kd_ref[b, rows, hs][None], cd_ref[b, ci:ci + 1, hs][None])
                s_scr[b, h] = s_new[0]
                o_ref[b, rows, hs] = _rms(o[0], dnorm) * _silu(z_ref[b, rows, hs])

    @pl.when(pl.program_id(0) == pl.num_programs(0) - 1)
    def _():
        sfin_ref[...] = s_scr[...]


def _delta_scan(u, wk, qd, kd, qk, cd, z, s0, dnorm, tt):
    b, s, _ = u.shape
    n = tt // DELTA_CHUNK
    tok = lambda w: pl.BlockSpec((b, tt, w), lambda j: (0, j, 0))
    st = pl.BlockSpec(s0.shape, lambda j: (0, 0, 0, 0))
    return pl.pallas_call(
        _delta_scan_kernel,
        grid=(s // tt,),
        in_specs=[tok(GROUP_WIDTH), tok(GROUP_WIDTH), tok(GROUP_WIDTH), tok(GROUP_WIDTH),
                  tok(N_HEADS * DELTA_CHUNK), pl.BlockSpec((b, n, GROUP_WIDTH), lambda j: (0, j, 0)),
                  tok(GROUP_WIDTH), st, _resident((1, HEAD_DIM))],
        out_specs=[tok(GROUP_WIDTH), st],
        out_shape=[jax.ShapeDtypeStruct((b, s, GROUP_WIDTH), F32), jax.ShapeDtypeStruct(s0.shape, F32)],
        scratch_shapes=[pltpu.VMEM(s0.shape, F32)],
        compiler_params=_cparams(("arbitrary",)),
        name="delta_scan",
    )(u, wk, qd, kd, qk, cd, z, s0, dnorm)


def _topk_blocks(gate, valid):
    m, nb = gate.shape
    blk = lax.broadcasted_iota(jnp.int32, (m, nb), 1).astype(F32)
    gv = jnp.where(valid, gate, -jnp.inf)
    picks = []
    for _ in range(MOBA_TOPK):
        top = jnp.max(gv, axis=1, keepdims=True)
        first = jnp.min(jnp.where(gv == top, blk, float(nb)), axis=1, keepdims=True)
        ok = top > -jnp.inf
        hit = (blk == first) & ok
        gv = jnp.where(hit, -jnp.inf, gv)
        picks.append((first, ok, hit))
    return picks


def _moba_prompt_kernel(q_ref, kmean_ref, k_ref, v_ref, o_ref):
    cur = pl.program_id(1)
    tq = q_ref.shape[1]
    nb = kmean_ref.shape[1]
    scale = HEAD_DIM ** -0.5
    row = lax.broadcasted_iota(jnp.int32, (tq, MOBA_BLOCK), 0)
    col = lax.broadcasted_iota(jnp.int32, (tq, MOBA_BLOCK), 1)
    blk_ids = lax.broadcasted_iota(jnp.int32, (tq, nb), 1)
    expand_rows = lax.broadcasted_iota(jnp.int32, (nb, MOBA_BLOCK), 0)

    for h in range(N_HEADS):
        hs = slice(h * HEAD_DIM, (h + 1) * HEAD_DIM)
        q = q_ref[0, :, hs]
        gate = _dot_nt(q, kmean_ref[0, :, hs], HI)
        sel = jnp.zeros((tq, nb), F32)
        for _, _, hit in _topk_blocks(gate, blk_ids < cur):
            sel = jnp.where(hit, 1.0, sel)
        sel_bf = sel.astype(BF16)
        q_bf = q.astype(BF16)

        def attend(carry, jb, mask_fn):
            m_i, l_i, acc = carry
            r0 = pl.multiple_of(jb * MOBA_BLOCK, MOBA_BLOCK)
            s = _dot_nt(q_bf, k_ref[0, pl.ds(r0, MOBA_BLOCK), hs]) * scale
            s = jnp.where(mask_fn(jb), s, NEG)
            m_new = jnp.maximum(m_i, jnp.max(s, axis=1, keepdims=True))
            alpha = jnp.exp(m_i - m_new)
            p = jnp.exp(s - m_new)
            l_new = alpha * l_i + jnp.sum(p, axis=1, keepdims=True)
            acc = alpha * acc + _dot(p.astype(BF16), v_ref[0, pl.ds(r0, MOBA_BLOCK), hs])
            return m_new, l_new, acc

        def selected(jb):
            onehot = jnp.where(expand_rows == jb, 1.0, 0.0).astype(BF16)
            return _dot(sel_bf, onehot) > 0.5

        init = (jnp.full((tq, 1), NEG, F32), jnp.zeros((tq, 1), F32), jnp.zeros((tq, HEAD_DIM), F32))
        carry = lax.fori_loop(0, cur, lambda jb, c: attend(c, jb, selected), init)
        _, l_i, acc = attend(carry, cur, lambda jb: col <= row)
        o_ref[0, :, hs] = acc / l_i


def _moba_prompt(q, kmean, k_bf, v_bf):
    b, s, _ = q.shape
    nb = s // MOBA_BLOCK
    tok = pl.BlockSpec((1, MOBA_BLOCK, GROUP_WIDTH), lambda bi, j: (bi, j, 0))
    seq = lambda a: pl.BlockSpec((1,) + a.shape[1:], lambda bi, j: (bi, 0, 0))
    return pl.pallas_call(
        _moba_prompt_kernel,
        grid=(b, nb),
        in_specs=[tok, seq(kmean), seq(k_bf), seq(v_bf)],
        out_specs=tok,
        out_shape=jax.ShapeDtypeStruct((b, s, GROUP_WIDTH), F32),
        compiler_params=_cparams(("arbitrary", "arbitrary")),
        name="moba_prompt",
    )(q, kmean, k_bf, v_bf)


def _post_kernel(x_ref, od_ref, om_ref, ple_ref, wout_ref, fnorm_ref, wg_ref, wu_ref, wd_ref,
                 wple_ref, wpg_ref, onorm_ref, y_ref, *, ff_chunk):
    mix = jnp.concatenate([od_ref[...], om_ref[...]], axis=1).astype(BF16)
    h = x_ref[...] + _dot(mix, wout_ref[...])
    f = _rms(h, fnorm_ref[...]).astype(BF16)
    d_ff = wg_ref.shape[1]
    ffn = None
    for c0 in range(0, d_ff, ff_chunk):
        gate = _dot(f, wg_ref[:, c0:c0 + ff_chunk])
        up = _dot(f, wu_ref[:, c0:c0 + ff_chunk])
        part = _dot((_silu(gate) * up).astype(BF16), wd_ref[c0:c0 + ff_chunk, :])
        ffn = part if ffn is None else ffn + part
    h = h + ffn
    h = h + _dot(ple_ref[...].astype(BF16), wple_ref[...]) * _sigmoid(_dot(h.astype(BF16), wpg_ref[...]))
    y_ref[...] = _rms(h, onorm_ref[...])


def _post(x, o_delta, o_moba, ple, w_out, ffn_norm, w_gate, w_up, w_down, w_ple, w_ple_gate, out_norm, tm):
    t, d = x.shape
    d_ff = w_gate.shape[1]
    ff_chunk = d_ff // 2 if (d_ff // 2) % LANE == 0 else d_ff
    tok = lambda w: pl.BlockSpec((tm, w), lambda i: (i, 0))
    return pl.pallas_call(
        functools.partial(_post_kernel, ff_chunk=ff_chunk),
        grid=(t // tm,),
        in_specs=[tok(d), tok(GROUP_WIDTH), tok(GROUP_WIDTH), tok(ple.shape[1]),
                  _resident(w_out.shape), _resident((1, d)), _resident(w_gate.shape), _resident(w_up.shape),
                  _resident(w_down.shape), _resident(w_ple.shape), _resident(w_ple_gate.shape), _resident((1, d))],
        out_specs=tok(d),
        out_shape=jax.ShapeDtypeStruct((t, d), F32),
        compiler_params=_cparams(("arbitrary",)),
        name="post",
    )(x, o_delta, o_moba, ple, w_out, ffn_norm, w_gate, w_up, w_down, w_ple, w_ple_gate, out_norm)


def _delta_sample_kernel(xx_ref, gate_ref, z_ref, s_ref, convw_ref, alog_ref, dtb_ref, dnorm_ref,
                         o_ref, snew_ref, *, n_tok):
    nseq = xx_ref.shape[0]
    rows = nseq * SUBLANE
    xx = xx_ref[...].reshape(rows, CONV_CH)
    w = convw_ref[...]
    y = xx * w[0:1]
    for j in range(1, CONV_WIDTH):
        y = y + pltpu.roll(xx, rows - j, axis=0) * w[j:j + 1]
    y = _silu(y)
    gates = gate_ref[...].reshape(rows, LANE)
    real = (lax.broadcasted_iota(jnp.int32, (rows, 1), 0) % SUBLANE) < n_tok
    dnorm = dnorm_ref[...]
    for h in range(N_HEADS):
        q, k, v, g, beta = _delta_inputs(y, gates, alog_ref[...], dtb_ref[...], h)
        g = jnp.where(real, g, 0.0)
        beta = jnp.where(real, beta, 0.0)
        r3 = lambda t: t.reshape(nseq, SUBLANE, t.shape[-1])
        o, s_new = _chunk_apply(s_ref[:, h], *_chunk_prep(r3(q), r3(k), r3(v), r3(g), r3(beta)))
        snew_ref[:, h] = s_new
        hs = slice(h * HEAD_DIM, (h + 1) * HEAD_DIM)
        o_ref[:, :, hs] = _rms(o, dnorm) * _silu(z_ref[:, :, hs])


def _delta_sample(xx, gates8, z8, state, conv_w, alog_row, dtb_row, dnorm, n_tok, nseq):
    nb = xx.shape[0]
    seq = lambda w: pl.BlockSpec((nseq, SUBLANE, w), lambda i: (i, 0, 0))
    st = pl.BlockSpec((nseq,) + state.shape[1:], lambda i: (i, 0, 0, 0))
    return pl.pallas_call(
        functools.partial(_delta_sample_kernel, n_tok=n_tok),
        grid=(nb // nseq,),
        in_specs=[seq(CONV_CH), seq(LANE), seq(GROUP_WIDTH), st, _resident(conv_w.shape),
                  _resident((1, LANE)), _resident((1, LANE)), _resident((1, HEAD_DIM))],
        out_specs=[seq(GROUP_WIDTH), st],
        out_shape=[jax.ShapeDtypeStruct((nb, SUBLANE, GROUP_WIDTH), F32), jax.ShapeDtypeStruct(state.shape, F32)],
        compiler_params=_cparams(("arbitrary",)),
        name="delta_sample",
    )(xx, gates8, z8, state, conv_w, alog_row, dtb_row, dnorm)


PAGES_PER_GROUP = 16


def _kmean_select_kernel(pt_ref, q_ref, kc_ref, sel_ref, buf, sem, kmean, *, n_pages, page_size):
    b = pl.program_id(0)
    nseq = pl.num_programs(0)
    n_groups = n_pages // PAGES_PER_GROUP
    ppb = MOBA_BLOCK // page_size
    blocks_per_group = PAGES_PER_GROUP // ppb

    def copies(seq, grp, slot):
        return [pltpu.make_async_copy(kc_ref.at[pt_ref[seq * n_pages + grp * PAGES_PER_GROUP + i]],
                                      buf.at[slot, i], sem.at[slot]) for i in range(PAGES_PER_GROUP)]

    @pl.when(b == 0)
    def _():
        for cp in copies(0, 0, 0):
            cp.start()

    for grp in range(n_groups):
        slot = (b * n_groups + grp) % 2
        if grp + 1 < n_groups:
            for cp in copies(b, grp + 1, 1 - slot):
                cp.start()
        else:
            @pl.when(b + 1 < nseq)
            def _():
                for cp in copies(b + 1, 0, 1 - slot):
                    cp.start()
        for cp in copies(b, grp, slot):
            cp.wait()
        pages = buf[slot].reshape(blocks_per_group, ppb * page_size, GROUP_WIDTH)
        kmean[grp * blocks_per_group:(grp + 1) * blocks_per_group, :] = jnp.sum(pages, axis=1) * (1.0 / MOBA_BLOCK)

    nb = kmean.shape[0]
    lane = lax.broadcasted_iota(jnp.int32, (SUBLANE, LANE), 1)
    out = jnp.zeros((SUBLANE, LANE), F32)
    for h in range(N_HEADS):
        hs = slice(h * HEAD_DIM, (h + 1) * HEAD_DIM)
        gate = _dot_nt(q_ref[0, :, hs], kmean[:, hs], HI)
        for r, (first, _, _) in enumerate(_topk_blocks(gate, jnp.full((SUBLANE, nb), True))):
            out = jnp.where(lane == h * MOBA_TOPK + r, first, out)
    sel_ref[0] = out.astype(jnp.int32)


def _kmean_select(pt_flat, q8, cache_k3, n_pages):
    nb = q8.shape[0]
    page_size = cache_k3.shape[1]
    n_blocks = n_pages * page_size // MOBA_BLOCK
    return pl.pallas_call(
        functools.partial(_kmean_select_kernel, n_pages=n_pages, page_size=page_size),
        grid_spec=pltpu.PrefetchScalarGridSpec(
            num_scalar_prefetch=1,
            grid=(nb,),
            in_specs=[pl.BlockSpec((1, SUBLANE, GROUP_WIDTH), lambda i, pt: (i, 0, 0)),
                      pl.BlockSpec(memory_space=pl.ANY)],
            out_specs=pl.BlockSpec((1, SUBLANE, LANE), lambda i, pt: (i, 0, 0)),
            scratch_shapes=[pltpu.VMEM((2, PAGES_PER_GROUP, page_size, GROUP_WIDTH), F32),
                            pltpu.SemaphoreType.DMA((2,)),
                            pltpu.VMEM((n_blocks, GROUP_WIDTH), F32)]),
        out_shape=jax.ShapeDtypeStruct((nb, SUBLANE, LANE), jnp.int32),
        compiler_params=_cparams(("arbitrary",)),
        name="kmean_select",
    )(pt_flat, q8, cache_k3)


def _moba_sample_kernel(pt_ref, sel_ref, q_ref, k_ref, v_ref, kc_ref, vc_ref, o_ref, kbuf, vbuf, sem,
                        *, n_tok, n_pages, page_size):
    b = pl.program_id(0)
    nseq = pl.num_programs(0)
    ppb = MOBA_BLOCK // page_size
    n_sel = N_HEADS * MOBA_TOPK
    scale = HEAD_DIM ** -0.5

    def copies(seq, slot):
        out = []
        for l in range(n_tok):
            for h in range(N_HEADS):
                for r in range(MOBA_TOPK):
                    blk = sel_ref[(seq * n_tok + l) * n_sel + h * MOBA_TOPK + r]
                    for p in range(ppb):
                        page = pt_ref[seq * n_pages + blk * ppb + p]
                        dst = ((l * N_HEADS + h) * MOBA_TOPK + r) * ppb + p
                        src = pl.ds(h * HEAD_DIM, HEAD_DIM)
                        out.append(pltpu.make_async_copy(kc_ref.at[page, :, src], kbuf.at[slot, dst], sem.at[0, slot]))
                        out.append(pltpu.make_async_copy(vc_ref.at[page, :, src], vbuf.at[slot, dst], sem.at[1, slot]))
        return out

    @pl.when(b == 0)
    def _():
        for cp in copies(0, 0):
            cp.start()

    slot = b % 2

    @pl.when(b + 1 < nseq)
    def _():
        for cp in copies(b + 1, 1 - slot):
            cp.start()

    for cp in copies(b, slot):
        cp.wait()

    row = lax.broadcasted_iota(jnp.int32, (SUBLANE, SUBLANE), 0)
    col = lax.broadcasted_iota(jnp.int32, (SUBLANE, SUBLANE), 1)
    own_mask = (col <= row) & (col < n_tok)
    out_row = lax.broadcasted_iota(jnp.int32, (SUBLANE, HEAD_DIM), 0)
    per_q = MOBA_TOPK * ppb
    for h in range(N_HEADS):
        hs = slice(h * HEAD_DIM, (h + 1) * HEAD_DIM)
        q_bf = q_ref[0, :, hs].astype(BF16)
        s_own = jnp.where(own_mask, _dot_nt(q_bf, k_ref[0, :, hs].astype(BF16)) * scale, NEG)
        m_own = jnp.max(s_own, axis=1, keepdims=True)
        v_own = v_ref[0, :, hs].astype(BF16)
        o_h = jnp.zeros((SUBLANE, HEAD_DIM), F32)
        for l in range(n_tok):
            base = (l * N_HEADS + h) * per_q
            k_sel = kbuf[slot, base:base + per_q].reshape(per_q * page_size, HEAD_DIM).astype(BF16)
            v_sel = vbuf[slot, base:base + per_q].reshape(per_q * page_size, HEAD_DIM).astype(BF16)
            s_sel = _dot_nt(q_bf, k_sel) * scale
            m = jnp.maximum(m_own, jnp.max(s_sel, axis=1, keepdims=True))
            p_sel = jnp.exp(s_sel - m)
            p_own = jnp.exp(s_own - m)
            denom = jnp.sum(p_sel, axis=1, keepdims=True) + jnp.sum(p_own, axis=1, keepdims=True)
            o_l = (_dot(p_sel.astype(BF16), v_sel) + _dot(p_own.astype(BF16), v_own)) / denom
            o_h = jnp.where(out_row == l, o_l, o_h)
        o_ref[0, :, hs] = o_h


def _moba_sample(pt_flat, sel_flat, q8, k8, v8, cache_k3, cache_v3, n_tok, n_pages):
    nb = q8.shape[0]
    page_size = cache_k3.shape[1]
    n_slabs = n_tok * N_HEADS * MOBA_TOPK * (MOBA_BLOCK // page_size)
    seq = pl.BlockSpec((1, SUBLANE, GROUP_WIDTH), lambda i, pt, sel: (i, 0, 0))
    return pl.pallas_call(
        functools.partial(_moba_sample_kernel, n_tok=n_tok, n_pages=n_pages, page_size=page_size),
        grid_spec=pltpu.PrefetchScalarGridSpec(
            num_scalar_prefetch=2,
            grid=(nb,),
            in_specs=[seq, seq, seq, pl.BlockSpec(memory_space=pl.ANY), pl.BlockSpec(memory_space=pl.ANY)],
            out_specs=seq,
            scratch_shapes=[pltpu.VMEM((2, n_slabs, page_size, HEAD_DIM), F32),
                            pltpu.VMEM((2, n_slabs, page_size, HEAD_DIM), F32),
                            pltpu.SemaphoreType.DMA((2, 2))]),
        out_shape=jax.ShapeDtypeStruct((nb, SUBLANE, GROUP_WIDTH), F32),
        compiler_params=_cparams(("arbitrary",)),
        name="moba_sample",
    )(pt_flat, sel_flat, q8, k8, v8, cache_k3, cache_v3)


def _rope_tables(pos):
    half = HEAD_DIM // 2
    inv_freq = jnp.exp(jnp.arange(half, dtype=F32) * (-2.0 * math.log(ROPE_THETA) / HEAD_DIM))
    ang = pos.astype(F32)[:, None] * inv_freq[None, :]
    cos, sin = jnp.cos(ang), jnp.sin(ang)
    return jnp.concatenate([cos, cos], axis=-1), jnp.concatenate([-sin, sin], axis=-1)


def _pad_rows(a, n_tok):
    return jnp.pad(a, ((0, 0), (0, SUBLANE - n_tok), (0, 0)))


def _pick_tile(n, pref):
    t = min(pref, n)
    while n % t:
        t //= 2
    return t


def kernel(x_prompt, x_sample, cache_k, cache_v, page_table, state_delta, state_conv, p_prompt, p_sample,
           attn_norm, w_in, conv_w, a_log, dt_bias, delta_norm, w_out, ffn_norm, w_gate, w_up, w_down,
           w_ple, w_ple_gate, final_norm):
    bp, s_len, d = x_prompt.shape
    bs, n_tok, _ = x_sample.shape
    depth = w_in.shape[0]
    n_pages = page_table.shape[1]
    page_size = cache_k.shape[2]
    assert depth == 1, "intermediate layers would need the un-normalised hidden state"
    assert n_tok <= SUBLANE - (CONV_WIDTH - 1) and s_len % MOBA_BLOCK == 0 and s_len >= SUBLANE
    assert (n_pages * page_size) % MOBA_BLOCK == 0 and n_pages % PAGES_PER_GROUP == 0
    assert n_pages * page_size // MOBA_BLOCK >= MOBA_TOPK and s_len // MOBA_BLOCK > MOBA_TOPK

    past = n_pages * page_size
    cos_p, sin_p = _rope_tables(jnp.arange(s_len, dtype=jnp.int32))
    cos_s, sin_s = _rope_tables(past + jnp.arange(n_tok, dtype=jnp.int32))
    cos_s, sin_s = jnp.tile(cos_s, (bs, 1)), jnp.tile(sin_s, (bs, 1))
    pt_flat = page_table.reshape(-1)
    row = lambda v: v.reshape(1, -1)

    hp = x_prompt.reshape(bp * s_len, d)
    hs = x_sample.reshape(bs * n_tok, d)
    tm_p = _pick_tile(s_len, 512)
    tm_s = _pick_tile(bs * n_tok, 128)
    tt = _pick_tile(s_len, SUBLANE * DELTA_CHUNK)
    outs = [[] for _ in range(8)]
    for i in range(depth):
        wi = w_in[i]
        c_dw = CONV_CH + GROUP_WIDTH
        w_in_r = jnp.concatenate(
            [wi[:, :c_dw], wi[:, c_dw + 2 * N_HEADS:], wi[:, c_dw:c_dw + 2 * N_HEADS],
             jnp.zeros((d, LANE - 2 * N_HEADS), F32)], axis=1).astype(BF16)
        alog_row = jnp.zeros((1, LANE), F32).at[0, N_HEADS:2 * N_HEADS].set(a_log[i])
        dtb_row = jnp.zeros((1, LANE), F32).at[0, N_HEADS:2 * N_HEADS].set(dt_bias[i])
        wb = lambda w: w[i].astype(BF16)
        post_w = (wb(w_out), row(ffn_norm[i]), wb(w_gate), wb(w_up), wb(w_down), wb(w_ple), wb(w_ple_gate))
        out_norm = row(final_norm)

        qkv, z, q_m, k_m, v_m, gates, k_bf, v_bf, ksum = _in_proj(hp, row(attn_norm[i]), w_in_r, cos_p, sin_p, tm_p)
        sh = lambda a: a.reshape(bp, s_len, a.shape[-1])
        qkv3 = sh(qkv)
        hist0 = jnp.zeros((bp, CONV_WIDTH - 1, CONV_CH), F32)
        u, wk, qd, kd, qk, cd = _delta_prep(qkv3, sh(gates), hist0, conv_w[i], alog_row, dtb_row, tt)
        s00 = jnp.zeros((bp, N_HEADS, HEAD_DIM, HEAD_DIM), F32)
        o_delta, d_p = _delta_scan(u, wk, qd, kd, qk, cd, sh(z), s00, row(delta_norm[i]), tt)
        nblk = s_len // MOBA_BLOCK
        kmean = ksum.reshape(bp, s_len // tm_p, -1, GROUP_WIDTH)
        kmean = kmean.reshape(bp, -1, GROUP_WIDTH)
        if tm_p < MOBA_BLOCK:
            kmean = kmean.reshape(bp, nblk, MOBA_BLOCK // tm_p, GROUP_WIDTH).sum(axis=2)
        kmean = kmean * (1.0 / MOBA_BLOCK)
        o_moba = _moba_prompt(sh(q_m), kmean, sh(k_bf), sh(v_bf))
        hp_new = _post(hp, o_delta.reshape(bp * s_len, GROUP_WIDTH), o_moba.reshape(bp * s_len, GROUP_WIDTH),
                       p_prompt[i].reshape(bp * s_len, -1), *post_w, out_norm, tm_p)
        outs[0].append(k_m.reshape(bp, s_len, N_HEADS, HEAD_DIM))
        outs[1].append(v_m.reshape(bp, s_len, N_HEADS, HEAD_DIM))
        outs[4].append(d_p)
        outs[6].append(qkv3[:, s_len - (CONV_WIDTH - 1):])
        hp = hp_new

        qkv, z, q_m, k_m, v_m, gates, _, _, _ = _in_proj(hs, row(attn_norm[i]), w_in_r, cos_s, sin_s, tm_s)
        sh = lambda a: a.reshape(bs, n_tok, a.shape[-1])
        qkv3 = sh(qkv)
        xx_full = jnp.concatenate([state_conv[i], qkv3], axis=1)
        xx = jnp.pad(xx_full, ((0, 0), (0, SUBLANE - xx_full.shape[1]), (0, 0)))
        nseq = _pick_tile(bs, 8)
        o_delta8, d_s = _delta_sample(xx, _pad_rows(sh(gates), n_tok), _pad_rows(sh(z), n_tok), state_delta[i],
                                      conv_w[i], alog_row, dtb_row, row(delta_norm[i]), n_tok, nseq)
        cache_k3 = cache_k[i].reshape(cache_k.shape[1], page_size, GROUP_WIDTH)
        cache_v3 = cache_v[i].reshape(cache_v.shape[1], page_size, GROUP_WIDTH)
        q8 = _pad_rows(sh(q_m), n_tok)
        sel = _kmean_select(pt_flat, q8, cache_k3, n_pages)
        sel_flat = sel[:, :n_tok, :N_HEADS * MOBA_TOPK].reshape(-1)
        o_moba8 = _moba_sample(pt_flat, sel_flat, q8, _pad_rows(sh(k_m), n_tok), _pad_rows(sh(v_m), n_tok),
                               cache_k3, cache_v3, n_tok, n_pages)
        hs_new = _post(hs, o_delta8[:, :n_tok].reshape(bs * n_tok, GROUP_WIDTH),
                       o_moba8[:, :n_tok].reshape(bs * n_tok, GROUP_WIDTH),
                       p_sample[i].reshape(bs * n_tok, -1), *post_w, out_norm, tm_s)
        outs[2].append(k_m.reshape(bs, n_tok, N_HEADS, HEAD_DIM))
        outs[3].append(v_m.reshape(bs, n_tok, N_HEADS, HEAD_DIM))
        outs[5].append(d_s)
        outs[7].append(xx_full[:, xx_full.shape[1] - (CONV_WIDTH - 1):])
        hs = hs_new

    y_prompt = hp.reshape(bp, s_len, d)
    y_sample = hs.reshape(bs, n_tok, d)
    st = [jnp.stack(o) for o in outs]
    return (y_prompt, y_sample, st[0], st[1], st[2], st[3], st[4], st[5], st[6], st[7])
```

```python
import functools
import math

import jax
import jax.numpy as jnp
from jax import lax
from jax.experimental import pallas as pl
from jax.experimental.pallas import tpu as pltpu

F32 = jnp.float32
BF16 = jnp.bfloat16
HI = lax.Precision.HIGHEST

HEAD_DIM = 128
N_HEADS = 4
GROUP_WIDTH = N_HEADS * HEAD_DIM
CONV_WIDTH = 4
CONV_CH = 3 * GROUP_WIDTH
DELTA_CHUNK = 64
MOBA_BLOCK = 256
MOBA_TOPK = 3
ROPE_THETA = 10000.0
RMS_EPS = 1e-6
L2_EPS = 1e-6
NEG = -1e30
LANE = 128
SUBLANE = 8
VMEM_LIMIT = 56 * 1024 * 1024

C_QKV = 0
C_Z = CONV_CH
C_QM = C_Z + GROUP_WIDTH
C_KM = C_QM + GROUP_WIDTH
C_VM = C_KM + GROUP_WIDTH
C_GATE = C_VM + GROUP_WIDTH
IN_COLS = C_GATE + LANE


def _cparams(sem):
    return pltpu.CompilerParams(dimension_semantics=sem, vmem_limit_bytes=VMEM_LIMIT)


def _resident(shape):
    nd = len(shape)
    return pl.BlockSpec(shape, lambda *_: (0,) * nd, pipeline_mode=pl.Buffered(1))


def _sigmoid(x):
    return 1.0 / (1.0 + jnp.exp(-x))


def _silu(x):
    return x * _sigmoid(x)


def _softplus(x):
    return jnp.maximum(x, 0.0) + jnp.log(1.0 + jnp.exp(-jnp.abs(x)))


def _rms(x, g):
    return x * lax.rsqrt(jnp.mean(x * x, axis=-1, keepdims=True) + RMS_EPS) * g


def _dot(a, b, prec=None):
    return jnp.dot(a, b, precision=prec, preferred_element_type=F32)


def _dot_nt(a, b, prec=None):
    return lax.dot_general(a, b, (((1,), (1,)), ((), ())), precision=prec, preferred_element_type=F32)


def _bmm(a, b, prec=None):
    return lax.dot_general(a, b, (((2,), (1,)), ((0,), (0,))), precision=prec, preferred_element_type=F32)


def _bmm_nt(a, b, prec=None):
    return lax.dot_general(a, b, (((2,), (2,)), ((0,), (0,))), precision=prec, preferred_element_type=F32)


def _bmm_tn(a, b, prec=None):
    return lax.dot_general(a, b, (((1,), (1,)), ((0,), (0,))), precision=prec, preferred_element_type=F32)


def _in_proj_kernel(x_ref, g_ref, w_ref, cos_ref, sin_ref,
                    qkv_ref, z_ref, q_ref, k_ref, v_ref, gate_ref, kbf_ref, vt_ref, ksum_ref):
    a = _rms(x_ref[...], g_ref[...]).astype(BF16)
    cos = cos_ref[...]
    sin = sin_ref[...]

    def proj(c0, width):
        return _dot(a, w_ref[:, c0:c0 + width])

    def rope(t):
        heads = []
        for h in range(N_HEADS):
            th = t[:, h * HEAD_DIM:(h + 1) * HEAD_DIM]
            heads.append(th * cos + pltpu.roll(th, HEAD_DIM // 2, axis=1) * sin)
        return jnp.concatenate(heads, axis=1)

    qkv_ref[...] = proj(C_QKV, CONV_CH)
    z_ref[...] = proj(C_Z, GROUP_WIDTH)
    q_ref[...] = rope(proj(C_QM, GROUP_WIDTH))
    k = rope(proj(C_KM, GROUP_WIDTH))
    v = proj(C_VM, GROUP_WIDTH)
    k_ref[...] = k
    v_ref[...] = v
    kbf_ref[...] = k.astype(BF16)
    vt_ref[0] = v.T.astype(BF16)
    nblk = ksum_ref.shape[1]
    ksum_ref[0] = jnp.sum(k.reshape(nblk, k.shape[0] // nblk, GROUP_WIDTH), axis=1)
    gate_ref[...] = proj(C_GATE, LANE)


def _in_proj(x, norm_w, w_in_r, cos, sin, tm):
    t, d = x.shape
    nt = t // tm
    tiles_per_seq = cos.shape[0] // tm
    n_tab = tiles_per_seq
    nblk = max(tm // MOBA_BLOCK, 1)
    tok = lambda w: pl.BlockSpec((tm, w), lambda i: (i, 0))
    tab = pl.BlockSpec((tm, HEAD_DIM), lambda i: (i % n_tab, 0))
    out_shapes = [
        jax.ShapeDtypeStruct((t, CONV_CH), F32), jax.ShapeDtypeStruct((t, GROUP_WIDTH), F32),
        jax.ShapeDtypeStruct((t, GROUP_WIDTH), F32), jax.ShapeDtypeStruct((t, GROUP_WIDTH), F32),
        jax.ShapeDtypeStruct((t, GROUP_WIDTH), F32), jax.ShapeDtypeStruct((t, LANE), F32),
        jax.ShapeDtypeStruct((t, GROUP_WIDTH), BF16),
        jax.ShapeDtypeStruct((nt // tiles_per_seq, GROUP_WIDTH, tiles_per_seq * tm), BF16),
        jax.ShapeDtypeStruct((nt, nblk, GROUP_WIDTH), F32),
    ]
    out_specs = [tok(CONV_CH), tok(GROUP_WIDTH), tok(GROUP_WIDTH), tok(GROUP_WIDTH), tok(GROUP_WIDTH),
                 tok(LANE), tok(GROUP_WIDTH),
                 pl.BlockSpec((1, GROUP_WIDTH, tm), lambda i: (i // tiles_per_seq, 0, i % tiles_per_seq)),
                 pl.BlockSpec((1, nblk, GROUP_WIDTH), lambda i: (i, 0, 0))]
    return pl.pallas_call(
        _in_proj_kernel,
        grid=(nt,),
        in_specs=[tok(d), _resident((1, d)), _resident(w_in_r.shape), tab, tab],
        out_specs=out_specs,
        out_shape=out_shapes,
        compiler_params=_cparams(("arbitrary",)),
        name="in_proj",
    )(x, norm_w, w_in_r, cos, sin)


def _split(x):
    hi = x.astype(BF16)
    return hi, (x - hi.astype(F32)).astype(BF16)


def _bmm_split(a, b, nt=False):
    ah, al = _split(a)
    bh, bl = _split(b)
    lhs = jnp.concatenate([ah, al, ah], axis=2)
    if nt:
        return _bmm_nt(lhs, jnp.concatenate([bh, bh, bl], axis=2))
    return _bmm(lhs, jnp.concatenate([bh, bh, bl], axis=1))


def _bmm_exact(a, b, nt=False):
    return _bmm_nt(a, b, HI) if nt else _bmm(a, b, HI)


def _chunk_prep(q, k, v, g, beta, mm):
    n, c, dh = q.shape
    row = lax.broadcasted_iota(jnp.int32, (c, c), 0)
    col = lax.broadcasted_iota(jnp.int32, (c, c), 1)
    causal = row >= col
    strict = row > col
    gx = jnp.concatenate([jnp.broadcast_to(g, (n, c, dh)),
                          jnp.where(strict, jnp.broadcast_to(g, (n, c, c)), 0.0)], axis=2)
    g1 = gx.astype(BF16)
    r1 = gx - g1.astype(F32)
    g2 = r1.astype(BF16)
    g3 = (r1 - g2.astype(F32)).astype(BF16)
    lower3 = jnp.broadcast_to(jnp.concatenate([causal.astype(BF16)] * 3, axis=1), (n, c, 3 * c))
    csum = _bmm(lower3, jnp.concatenate([g1, g2, g3], axis=1))
    gcb = csum[:, :, :dh]
    dlog = csum[:, :, dh:]
    decay = jnp.where(causal, jnp.exp(jnp.where(causal, dlog, 0.0)), 0.0)
    kb = k * beta
    kq = mm(jnp.concatenate([kb, q], axis=1), k, nt=True)
    a_mat = jnp.where(strict, kq[:, :c] * decay, 0.0)
    qk = jnp.where(causal, kq[:, c:] * decay, 0.0)
    nmat = -a_mat
    t_inv = jnp.where(row == col, 1.0, 0.0) + nmat
    for _ in range(int(math.log2(c)) - 1):
        nmat = mm(nmat, nmat)
        t_inv = t_inv + mm(t_inv, nmat)
    egc = jnp.exp(gcb)
    uw = mm(t_inv, jnp.concatenate([v * beta, kb * egc], axis=2))
    g_last = gcb[:, c - 1:c, :]
    return uw[:, :, :dh], uw[:, :, dh:], qk, q * egc, k * jnp.exp(g_last - gcb), jnp.exp(g_last)


def _chunk_apply(s, u, wk, qk, qd, kd, cd):
    c = u.shape[1]
    xs = _bmm(jnp.concatenate([wk, qd], axis=1).astype(BF16), s.astype(BF16))
    w = u - xs[:, :c]
    wb = w.astype(BF16)
    o = xs[:, c:] + _bmm(qk.astype(BF16), wb)
    s_new = s * cd + _bmm_tn(kd.astype(BF16), wb)
    return o, s_new


def _delta_inputs(y, gates, alog, dtb, h):
    sl = lambda base: slice(base + h * HEAD_DIM, base + (h + 1) * HEAD_DIM)
    q = y[:, sl(0)]
    k = y[:, sl(GROUP_WIDTH)]
    v = y[:, sl(2 * GROUP_WIDTH)]
    q = q * lax.rsqrt(jnp.sum(q * q, axis=-1, keepdims=True) + L2_EPS) * (HEAD_DIM ** -0.5)
    k = k * lax.rsqrt(jnp.sum(k * k, axis=-1, keepdims=True) + L2_EPS)
    beta = _sigmoid(gates[:, h:h + 1])
    g = -jnp.exp(alog[:, N_HEADS + h:N_HEADS + h + 1]) * _softplus(
        gates[:, N_HEADS + h:N_HEADS + h + 1] + dtb[:, N_HEADS + h:N_HEADS + h + 1])
    return q, k, v, g, beta


def _delta_prep_kernel(qkv_ref, gate_ref, hist_ref, convw_ref, alog_ref, dtb_ref,
                       u_ref, wk_ref, qd_ref, kd_ref, qk_ref, cd_ref, prev_ref):
    tt = qkv_ref.shape[1]
    n = tt // DELTA_CHUNK

    @pl.when(pl.program_id(1) == 0)
    def _():
        prev_ref[...] = jnp.zeros_like(prev_ref)
        prev_ref[SUBLANE - (CONV_WIDTH - 1):, :] = hist_ref[0]

    x = qkv_ref[0]
    full = jnp.concatenate([prev_ref[...], x], axis=0)
    w = convw_ref[...]
    base = SUBLANE - (CONV_WIDTH - 1)
    y = full[base:base + tt] * w[0:1]
    for j in range(1, CONV_WIDTH):
        y = y + full[base + j:base + j + tt] * w[j:j + 1]
    y = _silu(y)
    prev_ref[...] = x[tt - SUBLANE:]

    gates = gate_ref[0]
    alog = alog_ref[...]
    dtb = dtb_ref[...]
    for h in range(N_HEADS):
        q, k, v, g, beta = _delta_inputs(y, gates, alog, dtb, h)
        r3 = lambda t: t.reshape(n, DELTA_CHUNK, t.shape[-1])
        u, wk, qk, qd, kd, cd = _chunk_prep(r3(q), r3(k), r3(v), r3(g), r3(beta), _bmm_split)
        hs = slice(h * HEAD_DIM, (h + 1) * HEAD_DIM)
        u_ref[0, :, hs] = u.reshape(tt, HEAD_DIM)
        wk_ref[0, :, hs] = wk.reshape(tt, HEAD_DIM)
        qd_ref[0, :, hs] = qd.reshape(tt, HEAD_DIM)
        kd_ref[0, :, hs] = kd.reshape(tt, HEAD_DIM)
        qk_ref[0, :, h * DELTA_CHUNK:(h + 1) * DELTA_CHUNK] = qk.reshape(tt, DELTA_CHUNK)
        cd_ref[0, :, hs] = cd.reshape(n, HEAD_DIM)


def _delta_prep(qkv, gates, hist, conv_w, alog_row, dtb_row, tt):
    b, s, _ = qkv.shape
    n = tt // DELTA_CHUNK
    tok = lambda w: pl.BlockSpec((1, tt, w), lambda bi, j: (bi, j, 0))
    wide = jax.ShapeDtypeStruct((b, s, GROUP_WIDTH), F32)
    return pl.pallas_call(
        _delta_prep_kernel,
        grid=(b, s // tt),
        in_specs=[tok(CONV_CH), tok(LANE),
                  pl.BlockSpec((1, CONV_WIDTH - 1, CONV_CH), lambda bi, j: (bi, 0, 0)),
                  _resident(conv_w.shape), _resident((1, LANE)), _resident((1, LANE))],
        out_specs=[tok(GROUP_WIDTH), tok(GROUP_WIDTH), tok(GROUP_WIDTH), tok(GROUP_WIDTH),
                   tok(N_HEADS * DELTA_CHUNK),
                   pl.BlockSpec((1, n, GROUP_WIDTH), lambda bi, j: (bi, j, 0))],
        out_shape=[wide, wide, wide, wide,
                   jax.ShapeDtypeStruct((b, s, N_HEADS * DELTA_CHUNK), F32),
                   jax.ShapeDtypeStruct((b, s // DELTA_CHUNK, GROUP_WIDTH), F32)],
        scratch_shapes=[pltpu.VMEM((SUBLANE, CONV_CH), F32)],
        compiler_params=_cparams(("arbitrary", "arbitrary")),
        name="delta_prep",
    )(qkv, gates, hist, conv_w, alog_row, dtb_row)


def _delta_scan_kernel(u_ref, wk_ref, qd_ref, kd_ref, qk_ref, cd_ref, z_ref, s0_ref, dnorm_ref,
                       o_ref, sfin_ref, s_scr):
    nb, tt, _ = u_ref.shape
    n = tt // DELTA_CHUNK

    @pl.when(pl.program_id(0) == 0)
    def _():
        s_scr[...] = s0_ref[...]

    dnorm = dnorm_ref[...]
    for ci in range(n):
        rows = slice(ci * DELTA_CHUNK, (ci + 1) * DELTA_CHUNK)
        for b in range(nb):
            for h in range(N_HEADS):
                hs = slice(h * HEAD_DIM, (h + 1) * HEAD_DIM)
                o, s_new = _chunk_apply(
                    s_scr[b, h][None], u_ref[b, rows, hs][None], wk_ref[b, rows, hs][None],
                    qk_ref[b, rows, h * DELTA_CHUNK:(h + 1) * DELTA_CHUNK][None],
                    qd_ref[b, rows, hs][None], kd_ref[b, rows, hs][None], cd_ref[b, ci:ci + 1, hs][None])
                s_scr[b, h] = s_new[0]
                o_ref[b, rows, hs] = _rms(o[0], dnorm) * _silu(z_ref[b, rows, hs])

    @pl.when(pl.program_id(0) == pl.num_programs(0) - 1)
    def _():
        sfin_ref[...] = s_scr[...]


def _delta_scan(u, wk, qd, kd, qk, cd, z, s0, dnorm, tt):
    b, s, _ = u.shape
    n = tt // DELTA_CHUNK
    tok = lambda w: pl.BlockSpec((b, tt, w), lambda j: (0, j, 0))
    st = pl.BlockSpec(s0.shape, lambda j: (0, 0, 0, 0))
    return pl.pallas_call(
        _delta_scan_kernel,
        grid=(s // tt,),
        in_specs=[tok(GROUP_WIDTH), tok(GROUP_WIDTH), tok(GROUP_WIDTH), tok(GROUP_WIDTH),
                  tok(N_HEADS * DELTA_CHUNK), pl.BlockSpec((b, n, GROUP_WIDTH), lambda j: (0, j, 0)),
                  tok(GROUP_WIDTH), st, _resident((1, HEAD_DIM))],
        out_specs=[tok(GROUP_WIDTH), st],
        out_shape=[jax.ShapeDtypeStruct((b, s, GROUP_WIDTH), F32), jax.ShapeDtypeStruct(s0.shape, F32)],
        scratch_shapes=[pltpu.VMEM(s0.shape, F32)],
        compiler_params=_cparams(("arbitrary",)),
        name="delta_scan",
    )(u, wk, qd, kd, qk, cd, z, s0, dnorm)


def _topk_blocks(gate, valid, axis):
    nb = gate.shape[axis]
    blk = lax.broadcasted_iota(jnp.int32, gate.shape, axis).astype(F32)
    gv = jnp.where(valid, gate, -jnp.inf)
    hits = []
    for _ in range(MOBA_TOPK):
        top = jnp.max(gv, axis=axis, keepdims=True)
        first = jnp.min(jnp.where(gv == top, blk, float(nb)), axis=axis, keepdims=True)
        hit = (blk == first) & (top > -jnp.inf)
        gv = jnp.where(hit, -jnp.inf, gv)
        hits.append(hit)
    return hits


def _moba_prompt_kernel(q_ref, kmean_ref, k_ref, vt_ref, o_ref, bias_scr, qbf_scr, m_scr, l_scr, acc_scr):
    cur = pl.program_id(1)
    tq = q_ref.shape[1]
    nb = kmean_ref.shape[1]
    scale = HEAD_DIM ** -0.5
    key_i = lax.broadcasted_iota(jnp.int32, (MOBA_BLOCK, tq), 0)
    qry_i = lax.broadcasted_iota(jnp.int32, (MOBA_BLOCK, tq), 1)
    causal_bias = jnp.where(key_i <= qry_i, 0.0, NEG)
    blk_ids = lax.broadcasted_iota(jnp.int32, (nb, tq), 0)
    heads = [slice(h * HEAD_DIM, (h + 1) * HEAD_DIM) for h in range(N_HEADS)]

    for h, hs in enumerate(heads):
        q = q_ref[0, :, hs]
        gate_t = _dot_nt(kmean_ref[0, :, hs], q, HI)
        chosen = jnp.zeros((nb, tq), jnp.bool_)
        for hit in _topk_blocks(gate_t, blk_ids < cur, 0):
            chosen = chosen | hit
        bias_scr[h] = jnp.where(chosen, 0.0, NEG)
        qbf_scr[h] = (q * scale).astype(BF16)
        m_scr[h] = jnp.full((1, tq), NEG, F32)
        l_scr[h] = jnp.zeros((1, tq), F32)
        acc_scr[h] = jnp.zeros((HEAD_DIM, tq), F32)

    def attend(jb, bias_of):
        r0 = pl.multiple_of(jb * MOBA_BLOCK, MOBA_BLOCK)
        for h, hs in enumerate(heads):
            s = _dot_nt(k_ref[0, pl.ds(r0, MOBA_BLOCK), hs], qbf_scr[h]) + bias_of(h)
            m_i = m_scr[h]
            m_new = jnp.maximum(m_i, jnp.max(s, axis=0, keepdims=True))
            alpha = jnp.exp(m_i - m_new)
            p = jnp.exp(s - m_new)
            m_scr[h] = m_new
            l_scr[h] = alpha * l_scr[h] + jnp.sum(p, axis=0, keepdims=True)
            acc_scr[h] = alpha * acc_scr[h] + _dot(vt_ref[0, hs, pl.ds(r0, MOBA_BLOCK)], p.astype(BF16))

    def past_block(jb, carry):
        attend(jb, lambda h: bias_scr[h, pl.ds(jb, 1), :])
        return carry

    lax.fori_loop(0, cur, past_block, 0)
    attend(cur, lambda h: causal_bias)
    for h, hs in enumerate(heads):
        o_ref[0, :, hs] = (acc_scr[h] / l_scr[h]).T


def _moba_prompt(q, kmean, k_bf, vt_bf):
    b, s, _ = q.shape
    nb = s // MOBA_BLOCK
    tok = pl.BlockSpec((1, MOBA_BLOCK, GROUP_WIDTH), lambda bi, j: (bi, j, 0))
    seq = lambda a: pl.BlockSpec((1,) + a.shape[1:], lambda bi, j: (bi, 0, 0))
    return pl.pallas_call(
        _moba_prompt_kernel,
        grid=(b, nb),
        in_specs=[tok, seq(kmean), seq(k_bf), seq(vt_bf)],
        out_specs=tok,
        out_shape=jax.ShapeDtypeStruct((b, s, GROUP_WIDTH), F32),
        scratch_shapes=[pltpu.VMEM((N_HEADS, nb, MOBA_BLOCK), F32),
                        pltpu.VMEM((N_HEADS, MOBA_BLOCK, HEAD_DIM), BF16),
                        pltpu.VMEM((N_HEADS, 1, MOBA_BLOCK), F32),
                        pltpu.VMEM((N_HEADS, 1, MOBA_BLOCK), F32),
                        pltpu.VMEM((N_HEADS, HEAD_DIM, MOBA_BLOCK), F32)],
        compiler_params=_cparams(("arbitrary", "arbitrary")),
        name="moba_prompt",
    )(q, kmean, k_bf, vt_bf)


def _post_kernel(x_ref, od_ref, om_ref, ple_ref, wout_ref, fnorm_ref, wg_ref, wu_ref, wd_ref,
                 wple_ref, wpg_ref, onorm_ref, y_ref, *, ff_chunk):
    mix = jnp.concatenate([od_ref[...], om_ref[...]], axis=1).astype(BF16)
    h = x_ref[...] + _dot(mix, wout_ref[...])
    f = _rms(h, fnorm_ref[...]).astype(BF16)
    d_ff = wg_ref.shape[1]
    ffn = None
    for c0 in range(0, d_ff, ff_chunk):
        gate = _dot(f, wg_ref[:, c0:c0 + ff_chunk])
        up = _dot(f, wu_ref[:, c0:c0 + ff_chunk])
        part = _dot((_silu(gate) * up).astype(BF16), wd_ref[c0:c0 + ff_chunk, :])
        ffn = part if ffn is None else ffn + part
    h = h + ffn
    h = h + _dot(ple_ref[...].astype(BF16), wple_ref[...]) * _sigmoid(_dot(h.astype(BF16), wpg_ref[...]))
    y_ref[...] = _rms(h, onorm_ref[...])


def _post(x, o_delta, o_moba, ple, w_out, ffn_norm, w_gate, w_up, w_down, w_ple, w_ple_gate, out_norm, tm):
    t, d = x.shape
    d_ff = w_gate.shape[1]
    ff_chunk = d_ff // 2 if (d_ff // 2) % LANE == 0 else d_ff
    tok = lambda w: pl.BlockSpec((tm, w), lambda i: (i, 0))
    return pl.pallas_call(
        functools.partial(_post_kernel, ff_chunk=ff_chunk),
        grid=(t // tm,),
        in_specs=[tok(d), tok(GROUP_WIDTH), tok(GROUP_WIDTH), tok(ple.shape[1]),
                  _resident(w_out.shape), _resident((1, d)), _resident(w_gate.shape), _resident(w_up.shape),
                  _resident(w_down.shape), _resident(w_ple.shape), _resident(w_ple_gate.shape), _resident((1, d))],
        out_specs=tok(d),
        out_shape=jax.ShapeDtypeStruct((t, d), F32),
        compiler_params=_cparams(("arbitrary",)),
        name="post",
    )(x, o_delta, o_moba, ple, w_out, ffn_norm, w_gate, w_up, w_down, w_ple, w_ple_gate, out_norm)


def _delta_sample_kernel(xx_ref, gate_ref, z_ref, s_ref, convw_ref, alog_ref, dtb_ref, dnorm_ref,
                         o_ref, snew_ref, *, n_tok):
    nseq = xx_ref.shape[0]
    rows = nseq * SUBLANE
    xx = xx_ref[...].reshape(rows, CONV_CH)
    w = convw_ref[...]
    y = xx * w[0:1]
    for j in range(1, CONV_WIDTH):
        y = y + pltpu.roll(xx, rows - j, axis=0) * w[j:j + 1]
    y = _silu(y)
    gates = gate_ref[...].reshape(rows, LANE)
    real = (lax.broadcasted_iota(jnp.int32, (rows, 1), 0) % SUBLANE) < n_tok
    dnorm = dnorm_ref[...]
    for h in range(N_HEADS):
        q, k, v, g, beta = _delta_inputs(y, gates, alog_ref[...], dtb_ref[...], h)
        g = jnp.where(real, g, 0.0)
        beta = jnp.where(real, beta, 0.0)
        r3 = lambda t: t.reshape(nseq, SUBLANE, t.shape[-1])
        o, s_new = _chunk_apply(s_ref[:, h], *_chunk_prep(r3(q), r3(k), r3(v), r3(g), r3(beta), _bmm_exact))
        snew_ref[:, h] = s_new
        hs = slice(h * HEAD_DIM, (h + 1) * HEAD_DIM)
        o_ref[:, :, hs] = _rms(o, dnorm) * _silu(z_ref[:, :, hs])


def _delta_sample(xx, gates8, z8, state, conv_w, alog_row, dtb_row, dnorm, n_tok, nseq):
    nb = xx.shape[0]
    seq = lambda w: pl.BlockSpec((nseq, SUBLANE, w), lambda i: (i, 0, 0))
    st = pl.BlockSpec((nseq,) + state.shape[1:], lambda i: (i, 0, 0, 0))
    return pl.pallas_call(
        functools.partial(_delta_sample_kernel, n_tok=n_tok),
        grid=(nb // nseq,),
        in_specs=[seq(CONV_CH), seq(LANE), seq(GROUP_WIDTH), st, _resident(conv_w.shape),
                  _resident((1, LANE)), _resident((1, LANE)), _resident((1, HEAD_DIM))],
        out_specs=[seq(GROUP_WIDTH), st],
        out_shape=[jax.ShapeDtypeStruct((nb, SUBLANE, GROUP_WIDTH), F32), jax.ShapeDtypeStruct(state.shape, F32)],
        compiler_params=_cparams(("arbitrary",)),
        name="delta_sample",
    )(xx, gates8, z8, state, conv_w, alog_row, dtb_row, dnorm)


PAGES_PER_CHUNK = 16
N_SLOTS = 4


def _moba_sample_kernel(pt_ref, q_ref, kn_ref, vn_ref, kc_ref, vc_ref, o_ref,
                        buf, sem, s_scr, p_scr, psum, *, n_pages, page_size):
    b = pl.program_id(0)
    nseq = pl.num_programs(0)
    n_ck = n_pages // PAGES_PER_CHUNK
    n_stream = 2 * n_ck
    rows_per_page = page_size * N_HEADS
    cols = PAGES_PER_CHUNK * rows_per_page
    blk_cols = MOBA_BLOCK * N_HEADS
    n_blocks = n_pages * page_size // MOBA_BLOCK
    nq = q_ref.shape[1]
    scale = HEAD_DIM ** -0.5

    def copies(seq, j, slot):
        src = kc_ref if j < n_ck else vc_ref
        first = seq * n_pages + (j % n_ck) * PAGES_PER_CHUNK
        return [pltpu.make_async_copy(src.at[pt_ref[first + i]], buf.at[slot, i], sem.at[slot])
                for i in range(PAGES_PER_CHUNK)]

    def start_ahead(j):
        ahead = j + N_SLOTS - 1
        if ahead < n_stream:
            for cp in copies(b, ahead, ahead % N_SLOTS):
                cp.start()
        else:
            @pl.when(b + 1 < nseq)
            def _():
                for cp in copies(b + 1, ahead - n_stream, ahead % N_SLOTS):
                    cp.start()

    @pl.when(b == 0)
    def _():
        for j in range(N_SLOTS - 1):
            for cp in copies(0, j, j):
                cp.start()

    q = q_ref[0]
    q_bf = q.astype(BF16)

    for j in range(n_ck):
        slot = j % N_SLOTS
        for cp in copies(b, j, slot):
            cp.wait()
        start_ahead(j)
        pages = buf[slot]
        psum[j * PAGES_PER_CHUNK:(j + 1) * PAGES_PER_CHUNK] = jnp.sum(pages, axis=1)
        s_scr[:, j * cols:(j + 1) * cols] = _dot_nt(q_bf, pages.reshape(cols, HEAD_DIM).astype(BF16)) * scale

    prow = n_pages * 2 * N_HEADS
    gate_rows = _dot_nt(q, psum[...].reshape(prow, HEAD_DIM), HI)
    r_head = lax.broadcasted_iota(jnp.int32, (nq, prow), 0) % N_HEADS
    c_head = lax.broadcasted_iota(jnp.int32, (nq, prow), 1) % N_HEADS
    gate_rows = jnp.where(r_head == c_head, gate_rows, 0.0)
    rows_per_block = prow // n_blocks
    fold = jnp.where(lax.broadcasted_iota(jnp.int32, (prow, n_blocks), 0) // rows_per_block
                     == lax.broadcasted_iota(jnp.int32, (prow, n_blocks), 1), 1.0, 0.0)
    gate = _dot(gate_rows, fold, HI) * (1.0 / MOBA_BLOCK)
    sel = jnp.zeros((nq, n_blocks), F32)
    for hit in _topk_blocks(gate, jnp.full((nq, n_blocks), True), 1):
        sel = jnp.where(hit, 1.0, sel)

    head_ok = (lax.broadcasted_iota(jnp.int32, (nq, blk_cols), 0) % N_HEADS
               == lax.broadcasted_iota(jnp.int32, (nq, blk_cols), 1) % N_HEADS)
    orow = lax.broadcasted_iota(jnp.int32, (nq, nq), 0)
    ocol = lax.broadcasted_iota(jnp.int32, (nq, nq), 1)
    own_ok = (orow % N_HEADS == ocol % N_HEADS) & (ocol // N_HEADS <= orow // N_HEADS)
    s_own = jnp.where(own_ok, _dot_nt(q_bf, kn_ref[0].astype(BF16)) * scale, NEG)
    m = jnp.max(s_own, axis=1, keepdims=True)
    for n in range(n_blocks):
        cs = slice(n * blk_cols, (n + 1) * blk_cols)
        s_blk = jnp.where(head_ok & (sel[:, n:n + 1] > 0.5), s_scr[:, cs], NEG)
        s_scr[:, cs] = s_blk
        m = jnp.maximum(m, jnp.max(s_blk, axis=1, keepdims=True))
    p_own = jnp.exp(s_own - m)
    denom = jnp.sum(p_own, axis=1, keepdims=True)
    for n in range(n_blocks):
        cs = slice(n * blk_cols, (n + 1) * blk_cols)
        p_blk = jnp.exp(s_scr[:, cs] - m)
        denom = denom + jnp.sum(p_blk, axis=1, keepdims=True)
        p_scr[:, cs] = p_blk.astype(BF16)

    acc = _dot(p_own.astype(BF16), vn_ref[0].astype(BF16))
    for j in range(n_ck, n_stream):
        slot = j % N_SLOTS
        for cp in copies(b, j, slot):
            cp.wait()
        start_ahead(j)
        jc = j - n_ck
        acc = acc + _dot(p_scr[:, jc * cols:(jc + 1) * cols], buf[slot].reshape(cols, HEAD_DIM).astype(BF16))
    o_ref[0] = acc / denom


def _moba_sample(pt_flat, q16, kn16, vn16, cache_k4, cache_v4, n_pages):
    nb, nq, _ = q16.shape
    page_size = cache_k4.shape[1] * cache_k4.shape[2] // N_HEADS
    n_cols = n_pages * page_size * N_HEADS
    seq = pl.BlockSpec((1, nq, HEAD_DIM), lambda i, pt: (i, 0, 0))
    return pl.pallas_call(
        functools.partial(_moba_sample_kernel, n_pages=n_pages, page_size=page_size),
        grid_spec=pltpu.PrefetchScalarGridSpec(
            num_scalar_prefetch=1,
            grid=(nb,),
            in_specs=[seq, seq, seq, pl.BlockSpec(memory_space=pl.ANY), pl.BlockSpec(memory_space=pl.ANY)],
            out_specs=seq,
            scratch_shapes=[pltpu.VMEM((N_SLOTS, PAGES_PER_CHUNK) + cache_k4.shape[1:], F32),
                            pltpu.SemaphoreType.DMA((N_SLOTS,)),
                            pltpu.VMEM((nq, n_cols), F32),
                            pltpu.VMEM((nq, n_cols), BF16),
                            pltpu.VMEM((n_pages, 2 * N_HEADS, HEAD_DIM), F32)]),
        out_shape=jax.ShapeDtypeStruct((nb, nq, HEAD_DIM), F32),
        compiler_params=_cparams(("arbitrary",)),
        name="moba_sample",
    )(pt_flat, q16, kn16, vn16, cache_k4, cache_v4)


def _rope_tables(pos):
    half = HEAD_DIM // 2
    inv_freq = jnp.exp(jnp.arange(half, dtype=F32) * (-2.0 * math.log(ROPE_THETA) / HEAD_DIM))
    ang = pos.astype(F32)[:, None] * inv_freq[None, :]
    cos, sin = jnp.cos(ang), jnp.sin(ang)
    return jnp.concatenate([cos, cos], axis=-1), jnp.concatenate([-sin, sin], axis=-1)


def _pad_rows(a, n_tok):
    return jnp.pad(a, ((0, 0), (0, SUBLANE - n_tok), (0, 0)))


def _pick_tile(n, pref):
    t = min(pref, n)
    while n % t:
        t //= 2
    return t


def kernel(x_prompt, x_sample, cache_k, cache_v, page_table, state_delta, state_conv, p_prompt, p_sample,
           attn_norm, w_in, conv_w, a_log, dt_bias, delta_norm, w_out, ffn_norm, w_gate, w_up, w_down,
           w_ple, w_ple_gate, final_norm):
    bp, s_len, d = x_prompt.shape
    bs, n_tok, _ = x_sample.shape
    depth = w_in.shape[0]
    n_pages = page_table.shape[1]
    page_size = cache_k.shape[2]
    assert depth == 1, "intermediate layers would need the un-normalised hidden state"
    assert n_tok <= SUBLANE - (CONV_WIDTH - 1) and s_len % MOBA_BLOCK == 0 and s_len >= SUBLANE
    assert (n_pages * page_size) % MOBA_BLOCK == 0 and n_pages % PAGES_PER_CHUNK == 0 and page_size % 2 == 0
    assert (2 * n_pages // PAGES_PER_CHUNK) % N_SLOTS == 0
    assert n_pages * page_size // MOBA_BLOCK >= MOBA_TOPK and s_len // MOBA_BLOCK > MOBA_TOPK

    past = n_pages * page_size
    cos_p, sin_p = _rope_tables(jnp.arange(s_len, dtype=jnp.int32))
    cos_s, sin_s = _rope_tables(past + jnp.arange(n_tok, dtype=jnp.int32))
    cos_s, sin_s = jnp.tile(cos_s, (bs, 1)), jnp.tile(sin_s, (bs, 1))
    pt_flat = page_table.reshape(-1)
    row = lambda v: v.reshape(1, -1)

    hp = x_prompt.reshape(bp * s_len, d)
    hs = x_sample.reshape(bs * n_tok, d)
    tm_p = _pick_tile(s_len, 512)
    tm_s = _pick_tile(bs * n_tok, 128)
    tt = _pick_tile(s_len, SUBLANE * DELTA_CHUNK)
    outs = [[] for _ in range(8)]
    for i in range(depth):
        wi = w_in[i]
        c_dw = CONV_CH + GROUP_WIDTH
        w_in_r = jnp.concatenate(
            [wi[:, :c_dw], wi[:, c_dw + 2 * N_HEADS:], wi[:, c_dw:c_dw + 2 * N_HEADS],
             jnp.zeros((d, LANE - 2 * N_HEADS), F32)], axis=1).astype(BF16)
        alog_row = jnp.zeros((1, LANE), F32).at[0, N_HEADS:2 * N_HEADS].set(a_log[i])
        dtb_row = jnp.zeros((1, LANE), F32).at[0, N_HEADS:2 * N_HEADS].set(dt_bias[i])
        wb = lambda w: w[i].astype(BF16)
        post_w = (wb(w_out), row(ffn_norm[i]), wb(w_gate), wb(w_up), wb(w_down), wb(w_ple), wb(w_ple_gate))
        out_norm = row(final_norm)

        qkv, z, q_m, k_m, v_m, gates, k_bf, vt_bf, ksum = _in_proj(hp, row(attn_norm[i]), w_in_r, cos_p, sin_p, tm_p)
        sh = lambda a: a.reshape(bp, s_len, a.shape[-1])
        qkv3 = sh(qkv)
        hist0 = jnp.zeros((bp, CONV_WIDTH - 1, CONV_CH), F32)
        u, wk, qd, kd, qk, cd = _delta_prep(qkv3, sh(gates), hist0, conv_w[i], alog_row, dtb_row, tt)
        s00 = jnp.zeros((bp, N_HEADS, HEAD_DIM, HEAD_DIM), F32)
        o_delta, d_p = _delta_scan(u, wk, qd, kd, qk, cd, sh(z), s00, row(delta_norm[i]), tt)
        nblk = s_len // MOBA_BLOCK
        kmean = ksum.reshape(bp, s_len // tm_p, -1, GROUP_WIDTH)
        kmean = kmean.reshape(bp, -1, GROUP_WIDTH)
        if tm_p < MOBA_BLOCK:
            kmean = kmean.reshape(bp, nblk, MOBA_BLOCK // tm_p, GROUP_WIDTH).sum(axis=2)
        kmean = kmean * (1.0 / MOBA_BLOCK)
        o_moba = _moba_prompt(sh(q_m), kmean, sh(k_bf), vt_bf)
        hp_new = _post(hp, o_delta.reshape(bp * s_len, GROUP_WIDTH), o_moba.reshape(bp * s_len, GROUP_WIDTH),
                       p_prompt[i].reshape(bp * s_len, -1), *post_w, out_norm, tm_p)
        outs[0].append(k_m.reshape(bp, s_len, N_HEADS, HEAD_DIM))
        outs[1].append(v_m.reshape(bp, s_len, N_HEADS, HEAD_DIM))
        outs[4].append(d_p)
        outs[6].append(qkv3[:, s_len - (CONV_WIDTH - 1):])
        hp = hp_new

        qkv, z, q_m, k_m, v_m, gates, _, _, _ = _in_proj(hs, row(attn_norm[i]), w_in_r, cos_s, sin_s, tm_s)
        sh = lambda a: a.reshape(bs, n_tok, a.shape[-1])
        qkv3 = sh(qkv)
        xx_full = jnp.concatenate([state_conv[i], qkv3], axis=1)
        xx = jnp.pad(xx_full, ((0, 0), (0, SUBLANE - xx_full.shape[1]), (0, 0)))
        nseq = _pick_tile(bs, 8)
        o_delta8, d_s = _delta_sample(xx, _pad_rows(sh(gates), n_tok), _pad_rows(sh(z), n_tok), state_delta[i],
                                      conv_w[i], alog_row, dtb_row, row(delta_norm[i]), n_tok, nseq)
        n_pool = cache_k.shape[1]
        page_view = (n_pool, page_size // 2, 2 * N_HEADS, HEAD_DIM)
        rows = lambda a: a.reshape(bs, n_tok * N_HEADS, HEAD_DIM)
        o_moba_s = _moba_sample(pt_flat, rows(q_m), rows(k_m), rows(v_m),
                                cache_k[i].reshape(page_view), cache_v[i].reshape(page_view), n_pages)
        hs_new = _post(hs, o_delta8[:, :n_tok].reshape(bs * n_tok, GROUP_WIDTH),
                       o_moba_s.reshape(bs * n_tok, GROUP_WIDTH),
                       p_sample[i].reshape(bs * n_tok, -1), *post_w, out_norm, tm_s)
        outs[2].append(k_m.reshape(bs, n_tok, N_HEADS, HEAD_DIM))
        outs[3].append(v_m.reshape(bs, n_tok, N_HEADS, HEAD_DIM))
        outs[5].append(d_s)
        outs[7].append(xx_full[:, xx_full.shape[1] - (CONV_WIDTH - 1):])
        hs = hs_new

    y_prompt = hp.reshape(bp, s_len, d)
    y_sample = hs.reshape(bs, n_tok, d)
    st = [jnp.stack(o) for o in outs]
    return (y_prompt, y_sample, st[0], st[1], st[2], st[3], st[4], st[5], st[6], st[7])
```

```python
import functools
import math

import jax
import jax.numpy as jnp
from jax import lax
from jax.experimental import pallas as pl
from jax.experimental.pallas import tpu as pltpu

F32 = jnp.float32
BF16 = jnp.bfloat16
HI = lax.Precision.HIGHEST

HEAD_DIM = 128
N_HEADS = 4
GROUP_WIDTH = N_HEADS * HEAD_DIM
CONV_WIDTH = 4
CONV_CH = 3 * GROUP_WIDTH
DELTA_CHUNK = 64
MOBA_BLOCK = 256
MOBA_TOPK = 3
ROPE_THETA = 10000.0
RMS_EPS = 1e-6
L2_EPS = 1e-6
NEG = -1e30
LANE = 128
SUBLANE = 8
VMEM_LIMIT = 56 * 1024 * 1024

C_QKV = 0
C_Z = CONV_CH
C_QM = C_Z + GROUP_WIDTH
C_KM = C_QM + GROUP_WIDTH
C_VM = C_KM + GROUP_WIDTH
C_GATE = C_VM + GROUP_WIDTH
IN_COLS = C_GATE + LANE


def _cparams(sem):
    return pltpu.CompilerParams(dimension_semantics=sem, vmem_limit_bytes=VMEM_LIMIT)


def _resident(shape):
    nd = len(shape)
    return pl.BlockSpec(shape, lambda *_: (0,) * nd, pipeline_mode=pl.Buffered(1))


def _sigmoid(x):
    return 1.0 / (1.0 + jnp.exp(-x))


def _silu(x):
    return x * _sigmoid(x)


def _softplus(x):
    return jnp.maximum(x, 0.0) + jnp.log(1.0 + jnp.exp(-jnp.abs(x)))


def _rms(x, g):
    return x * lax.rsqrt(jnp.mean(x * x, axis=-1, keepdims=True) + RMS_EPS) * g


def _dot(a, b, prec=None):
    return jnp.dot(a, b, precision=prec, preferred_element_type=F32)


def _dot_nt(a, b, prec=None):
    return lax.dot_general(a, b, (((1,), (1,)), ((), ())), precision=prec, preferred_element_type=F32)


def _bmm(a, b, prec=None):
    return lax.dot_general(a, b, (((2,), (1,)), ((0,), (0,))), precision=prec, preferred_element_type=F32)


def _bmm_nt(a, b, prec=None):
    return lax.dot_general(a, b, (((2,), (2,)), ((0,), (0,))), precision=prec, preferred_element_type=F32)


def _bmm_tn(a, b, prec=None):
    return lax.dot_general(a, b, (((1,), (1,)), ((0,), (0,))), precision=prec, preferred_element_type=F32)


def _in_proj_kernel(x_ref, g_ref, w_ref, cos_ref, sin_ref,
                    qkv_ref, z_ref, q_ref, k_ref, v_ref, gate_ref, kbf_ref, vt_ref, ksum_ref):
    a = _rms(x_ref[...], g_ref[...]).astype(BF16)
    cos = cos_ref[...]
    sin = sin_ref[...]

    def proj(c0, width):
        return _dot(a, w_ref[:, c0:c0 + width])

    def rope(t):
        heads = []
        for h in range(N_HEADS):
            th = t[:, h * HEAD_DIM:(h + 1) * HEAD_DIM]
            heads.append(th * cos + pltpu.roll(th, HEAD_DIM // 2, axis=1) * sin)
        return jnp.concatenate(heads, axis=1)

    qkv_ref[...] = proj(C_QKV, CONV_CH)
    z_ref[...] = proj(C_Z, GROUP_WIDTH)
    q_ref[...] = rope(proj(C_QM, GROUP_WIDTH))
    k = rope(proj(C_KM, GROUP_WIDTH))
    v = proj(C_VM, GROUP_WIDTH)
    k_ref[...] = k
    v_ref[...] = v
    kbf_ref[...] = k.astype(BF16)
    vt_ref[0] = v.T.astype(BF16)
    nblk = ksum_ref.shape[1]
    ksum_ref[0] = jnp.sum(k.reshape(nblk, k.shape[0] // nblk, GROUP_WIDTH), axis=1)
    gate_ref[...] = proj(C_GATE, LANE)


def _in_proj(x, norm_w, w_in_r, cos, sin, tm):
    t, d = x.shape
    nt = t // tm
    tiles_per_seq = cos.shape[0] // tm
    n_tab = tiles_per_seq
    nblk = max(tm // MOBA_BLOCK, 1)
    tok = lambda w: pl.BlockSpec((tm, w), lambda i: (i, 0))
    tab = pl.BlockSpec((tm, HEAD_DIM), lambda i: (i % n_tab, 0))
    out_shapes = [
        jax.ShapeDtypeStruct((t, CONV_CH), F32), jax.ShapeDtypeStruct((t, GROUP_WIDTH), F32),
        jax.ShapeDtypeStruct((t, GROUP_WIDTH), F32), jax.ShapeDtypeStruct((t, GROUP_WIDTH), F32),
        jax.ShapeDtypeStruct((t, GROUP_WIDTH), F32), jax.ShapeDtypeStruct((t, LANE), F32),
        jax.ShapeDtypeStruct((t, GROUP_WIDTH), BF16),
        jax.ShapeDtypeStruct((nt // tiles_per_seq, GROUP_WIDTH, tiles_per_seq * tm), BF16),
        jax.ShapeDtypeStruct((nt, nblk, GROUP_WIDTH), F32),
    ]
    out_specs = [tok(CONV_CH), tok(GROUP_WIDTH), tok(GROUP_WIDTH), tok(GROUP_WIDTH), tok(GROUP_WIDTH),
                 tok(LANE), tok(GROUP_WIDTH),
                 pl.BlockSpec((1, GROUP_WIDTH, tm), lambda i: (i // tiles_per_seq, 0, i % tiles_per_seq)),
                 pl.BlockSpec((1, nblk, GROUP_WIDTH), lambda i: (i, 0, 0))]
    return pl.pallas_call(
        _in_proj_kernel,
        grid=(nt,),
        in_specs=[tok(d), _resident((1, d)), _resident(w_in_r.shape), tab, tab],
        out_specs=out_specs,
        out_shape=out_shapes,
        compiler_params=_cparams(("arbitrary",)),
        name="in_proj",
    )(x, norm_w, w_in_r, cos, sin)


def _split(x):
    hi = x.astype(BF16)
    return hi, (x - hi.astype(F32)).astype(BF16)


def _bmm_split(a, b, nt=False):
    ah, al = _split(a)
    bh, bl = _split(b)
    lhs = jnp.concatenate([ah, al, ah], axis=2)
    if nt:
        return _bmm_nt(lhs, jnp.concatenate([bh, bh, bl], axis=2))
    return _bmm(lhs, jnp.concatenate([bh, bh, bl], axis=1))


def _bmm_exact(a, b, nt=False):
    return _bmm_nt(a, b, HI) if nt else _bmm(a, b, HI)


def _chunk_prep(q, k, v, g, beta, mm):
    n, c, dh = q.shape
    row = lax.broadcasted_iota(jnp.int32, (c, c), 0)
    col = lax.broadcasted_iota(jnp.int32, (c, c), 1)
    causal = row >= col
    strict = row > col
    gx = jnp.concatenate([jnp.broadcast_to(g, (n, c, dh)),
                          jnp.where(strict, jnp.broadcast_to(g, (n, c, c)), 0.0)], axis=2)
    g1 = gx.astype(BF16)
    r1 = gx - g1.astype(F32)
    g2 = r1.astype(BF16)
    g3 = (r1 - g2.astype(F32)).astype(BF16)
    lower3 = jnp.broadcast_to(jnp.concatenate([causal.astype(BF16)] * 3, axis=1), (n, c, 3 * c))
    csum = _bmm(lower3, jnp.concatenate([g1, g2, g3], axis=1))
    gcb = csum[:, :, :dh]
    dlog = csum[:, :, dh:]
    decay = jnp.where(causal, jnp.exp(jnp.where(causal, dlog, 0.0)), 0.0)
    kb = k * beta
    kq = mm(jnp.concatenate([kb, q], axis=1), k, nt=True)
    a_mat = jnp.where(strict, kq[:, :c] * decay, 0.0)
    qk = jnp.where(causal, kq[:, c:] * decay, 0.0)
    nmat = -a_mat
    t_inv = jnp.where(row == col, 1.0, 0.0) + nmat
    for _ in range(int(math.log2(c)) - 1):
        nmat = mm(nmat, nmat)
        t_inv = t_inv + mm(t_inv, nmat)
    egc = jnp.exp(gcb)
    uw = mm(t_inv, jnp.concatenate([v * beta, kb * egc], axis=2))
    g_last = gcb[:, c - 1:c, :]
    return uw[:, :, :dh], uw[:, :, dh:], qk, q * egc, k * jnp.exp(g_last - gcb), jnp.exp(g_last)


def _chunk_apply(s, u, wk, qk, qd, kd, cd):
    c = u.shape[1]
    xs = _bmm(jnp.concatenate([wk, qd], axis=1).astype(BF16), s.astype(BF16))
    w = u - xs[:, :c]
    wb = w.astype(BF16)
    o = xs[:, c:] + _bmm(qk.astype(BF16), wb)
    s_new = s * cd + _bmm_tn(kd.astype(BF16), wb)
    return o, s_new


def _delta_inputs(y, gates, alog, dtb, h):
    sl = lambda base: slice(base + h * HEAD_DIM, base + (h + 1) * HEAD_DIM)
    q = y[:, sl(0)]
    k = y[:, sl(GROUP_WIDTH)]
    v = y[:, sl(2 * GROUP_WIDTH)]
    q = q * lax.rsqrt(jnp.sum(q * q, axis=-1, keepdims=True) + L2_EPS) * (HEAD_DIM ** -0.5)
    k = k * lax.rsqrt(jnp.sum(k * k, axis=-1, keepdims=True) + L2_EPS)
    beta = _sigmoid(gates[:, h:h + 1])
    g = -jnp.exp(alog[:, N_HEADS + h:N_HEADS + h + 1]) * _softplus(
        gates[:, N_HEADS + h:N_HEADS + h + 1] + dtb[:, N_HEADS + h:N_HEADS + h + 1])
    return q, k, v, g, beta


def _delta_prep_kernel(qkv_ref, gate_ref, hist_ref, convw_ref, alog_ref, dtb_ref,
                       u_ref, wk_ref, qd_ref, kd_ref, qk_ref, cd_ref, prev_ref):
    tt = qkv_ref.shape[1]
    n = tt // DELTA_CHUNK

    @pl.when(pl.program_id(1) == 0)
    def _():
        prev_ref[...] = jnp.zeros_like(prev_ref)
        prev_ref[SUBLANE - (CONV_WIDTH - 1):, :] = hist_ref[0]

    x = qkv_ref[0]
    full = jnp.concatenate([prev_ref[...], x], axis=0)
    w = convw_ref[...]
    base = SUBLANE - (CONV_WIDTH - 1)
    y = full[base:base + tt] * w[0:1]
    for j in range(1, CONV_WIDTH):
        y = y + full[base + j:base + j + tt] * w[j:j + 1]
    y = _silu(y)
    prev_ref[...] = x[tt - SUBLANE:]

    gates = gate_ref[0]
    alog = alog_ref[...]
    dtb = dtb_ref[...]
    for h in range(N_HEADS):
        q, k, v, g, beta = _delta_inputs(y, gates, alog, dtb, h)
        r3 = lambda t: t.reshape(n, DELTA_CHUNK, t.shape[-1])
        u, wk, qk, qd, kd, cd = _chunk_prep(r3(q), r3(k), r3(v), r3(g), r3(beta), _bmm_split)
        hs = slice(h * HEAD_DIM, (h + 1) * HEAD_DIM)
        u_ref[0, :, hs] = u.reshape(tt, HEAD_DIM)
        wk_ref[0, :, hs] = wk.reshape(tt, HEAD_DIM)
        qd_ref[0, :, hs] = qd.reshape(tt, HEAD_DIM)
        kd_ref[0, :, hs] = kd.reshape(tt, HEAD_DIM)
        qk_ref[0, :, h * DELTA_CHUNK:(h + 1) * DELTA_CHUNK] = qk.reshape(tt, DELTA_CHUNK)
        cd_ref[0, :, hs] = cd.reshape(n, HEAD_DIM)


def _delta_prep(qkv, gates, hist, conv_w, alog_row, dtb_row, tt):
    b, s, _ = qkv.shape
    n = tt // DELTA_CHUNK
    tok = lambda w: pl.BlockSpec((1, tt, w), lambda bi, j: (bi, j, 0))
    wide = jax.ShapeDtypeStruct((b, s, GROUP_WIDTH), F32)
    return pl.pallas_call(
        _delta_prep_kernel,
        grid=(b, s // tt),
        in_specs=[tok(CONV_CH), tok(LANE),
                  pl.BlockSpec((1, CONV_WIDTH - 1, CONV_CH), lambda bi, j: (bi, 0, 0)),
                  _resident(conv_w.shape), _resident((1, LANE)), _resident((1, LANE))],
        out_specs=[tok(GROUP_WIDTH), tok(GROUP_WIDTH), tok(GROUP_WIDTH), tok(GROUP_WIDTH),
                   tok(N_HEADS * DELTA_CHUNK),
                   pl.BlockSpec((1, n, GROUP_WIDTH), lambda bi, j: (bi, j, 0))],
        out_shape=[wide, wide, wide, wide,
                   jax.ShapeDtypeStruct((b, s, N_HEADS * DELTA_CHUNK), F32),
                   jax.ShapeDtypeStruct((b, s // DELTA_CHUNK, GROUP_WIDTH), F32)],
        scratch_shapes=[pltpu.VMEM((SUBLANE, CONV_CH), F32)],
        compiler_params=_cparams(("arbitrary", "arbitrary")),
        name="delta_prep",
    )(qkv, gates, hist, conv_w, alog_row, dtb_row)


def _delta_scan_kernel(u_ref, wk_ref, qd_ref, kd_ref, qk_ref, cd_ref, z_ref, s0_ref, dnorm_ref,
                       o_ref, sfin_ref, s_scr):
    nb, tt, _ = u_ref.shape
    n = tt // DELTA_CHUNK

    @pl.when(pl.program_id(0) == 0)
    def _():
        s_scr[...] = s0_ref[...]

    dnorm = dnorm_ref[...]
    for ci in range(n):
        rows = slice(ci * DELTA_CHUNK, (ci + 1) * DELTA_CHUNK)
        for b in range(nb):
            for h in range(N_HEADS):
                hs = slice(h * HEAD_DIM, (h + 1) * HEAD_DIM)
                o, s_new = _chunk_apply(
                    s_scr[b, h][None], u_ref[b, rows, hs][None], wk_ref[b, rows, hs][None],
                    qk_ref[b, rows, h * DELTA_CHUNK:(h + 1) * DELTA_CHUNK][None],
                    qd_ref[b, rows, hs][None], kd_ref[b, rows, hs][None], cd_ref[b, ci:ci + 1, hs][None])
                s_scr[b, h] = s_new[0]
                o_ref[b, rows, hs] = _rms(o[0], dnorm) * _silu(z_ref[b, rows, hs])

    @pl.when(pl.program_id(0) == pl.num_programs(0) - 1)
    def _():
        sfin_ref[...] = s_scr[...]


def _delta_scan(u, wk, qd, kd, qk, cd, z, s0, dnorm, tt):
    b, s, _ = u.shape
    n = tt // DELTA_CHUNK
    tok = lambda w: pl.BlockSpec((b, tt, w), lambda j: (0, j, 0))
    st = pl.BlockSpec(s0.shape, lambda j: (0, 0, 0, 0))
    return pl.pallas_call(
        _delta_scan_kernel,
        grid=(s // tt,),
        in_specs=[tok(GROUP_WIDTH), tok(GROUP_WIDTH), tok(GROUP_WIDTH), tok(GROUP_WIDTH),
                  tok(N_HEADS * DELTA_CHUNK), pl.BlockSpec((b, n, GROUP_WIDTH), lambda j: (0, j, 0)),
                  tok(GROUP_WIDTH), st, _resident((1, HEAD_DIM))],
        out_specs=[tok(GROUP_WIDTH), st],
        out_shape=[jax.ShapeDtypeStruct((b, s, GROUP_WIDTH), F32), jax.ShapeDtypeStruct(s0.shape, F32)],
        scratch_shapes=[pltpu.VMEM(s0.shape, F32)],
        compiler_params=_cparams(("arbitrary",)),
        name="delta_scan",
    )(u, wk, qd, kd, qk, cd, z, s0, dnorm)


def _topk_blocks(gate, valid, axis):
    nb = gate.shape[axis]
    blk = lax.broadcasted_iota(jnp.int32, gate.shape, axis).astype(F32)
    gv = jnp.where(valid, gate, -jnp.inf)
    hits = []
    for _ in range(MOBA_TOPK):
        top = jnp.max(gv, axis=axis, keepdims=True)
        first = jnp.min(jnp.where(gv == top, blk, float(nb)), axis=axis, keepdims=True)
        hit = (blk == first) & (top > -jnp.inf)
        gv = jnp.where(hit, -jnp.inf, gv)
        hits.append(hit)
    return hits


def _moba_prompt_kernel(q_ref, kmean_ref, k_ref, vt_ref, o_ref,
                        bias_scr, qbf_scr, m_scr, l_scr, acc_scr, sa_scr, sb_scr):
    cur = pl.program_id(1)
    tq = q_ref.shape[1]
    nb = kmean_ref.shape[1]
    scale = HEAD_DIM ** -0.5
    key_i = lax.broadcasted_iota(jnp.int32, (MOBA_BLOCK, tq), 0)
    qry_i = lax.broadcasted_iota(jnp.int32, (MOBA_BLOCK, tq), 1)
    causal_bias = jnp.where(key_i <= qry_i, 0.0, NEG)
    blk_ids = lax.broadcasted_iota(jnp.int32, (nb, tq), 0)
    heads = [slice(h * HEAD_DIM, (h + 1) * HEAD_DIM) for h in range(N_HEADS)]

    for h, hs in enumerate(heads):
        q = q_ref[0, :, hs]
        gate_t = _dot_nt(kmean_ref[0, :, hs], q, HI)
        chosen = jnp.zeros((nb, tq), jnp.bool_)
        for hit in _topk_blocks(gate_t, blk_ids < cur, 0):
            chosen = chosen | hit
        bias_scr[h] = jnp.where(chosen, 0.0, NEG)
        qbf_scr[h] = (q * scale).astype(BF16)
        m_scr[h] = jnp.full((1, tq), NEG, F32)
        l_scr[h] = jnp.zeros((1, tq), F32)
        acc_scr[h] = jnp.zeros((HEAD_DIM, tq), F32)

    def scores(jb, s_buf):
        r0 = pl.multiple_of(jb * MOBA_BLOCK, MOBA_BLOCK)
        for h, hs in enumerate(heads):
            s_buf[h] = _dot_nt(k_ref[0, pl.ds(r0, MOBA_BLOCK), hs], qbf_scr[h])

    def update(jb, s_buf, bias_of):
        r0 = pl.multiple_of(jb * MOBA_BLOCK, MOBA_BLOCK)
        for h, hs in enumerate(heads):
            s = s_buf[h] + bias_of(h)
            m_i = m_scr[h]
            m_new = jnp.maximum(m_i, jnp.max(s, axis=0, keepdims=True))
            alpha = jnp.exp(m_i - m_new)
            p = jnp.exp(s - m_new)
            m_scr[h] = m_new
            l_scr[h] = alpha * l_scr[h] + jnp.sum(p, axis=0, keepdims=True)
            acc_scr[h] = alpha * acc_scr[h] + _dot(vt_ref[0, hs, pl.ds(r0, MOBA_BLOCK)], p.astype(BF16))

    def chosen_bias(jb):
        return lambda h: bias_scr[h, pl.ds(jb, 1), :]

    def own_bias(h):
        return causal_bias

    scores(0, sa_scr)

    def block_pair(pair, carry):
        jb = 2 * pair
        scores(jb + 1, sb_scr)
        update(jb, sa_scr, chosen_bias(jb))
        scores(jb + 2, sa_scr)
        update(jb + 1, sb_scr, chosen_bias(jb + 1))
        return carry

    lax.fori_loop(0, cur // 2, block_pair, 0)

    @pl.when(cur % 2 == 1)
    def _():
        scores(cur, sb_scr)
        update(cur - 1, sa_scr, chosen_bias(cur - 1))
        update(cur, sb_scr, own_bias)

    @pl.when(cur % 2 == 0)
    def _():
        update(cur, sa_scr, own_bias)

    for h, hs in enumerate(heads):
        o_ref[0, :, hs] = (acc_scr[h] / l_scr[h]).T


def _moba_prompt(q, kmean, k_bf, vt_bf):
    b, s, _ = q.shape
    nb = s // MOBA_BLOCK
    tok = pl.BlockSpec((1, MOBA_BLOCK, GROUP_WIDTH), lambda bi, j: (bi, j, 0))
    seq = lambda a: pl.BlockSpec((1,) + a.shape[1:], lambda bi, j: (bi, 0, 0))
    return pl.pallas_call(
        _moba_prompt_kernel,
        grid=(b, nb),
        in_specs=[tok, seq(kmean), seq(k_bf), seq(vt_bf)],
        out_specs=tok,
        out_shape=jax.ShapeDtypeStruct((b, s, GROUP_WIDTH), F32),
        scratch_shapes=[pltpu.VMEM((N_HEADS, nb, MOBA_BLOCK), F32),
                        pltpu.VMEM((N_HEADS, MOBA_BLOCK, HEAD_DIM), BF16),
                        pltpu.VMEM((N_HEADS, 1, MOBA_BLOCK), F32),
                        pltpu.VMEM((N_HEADS, 1, MOBA_BLOCK), F32),
                        pltpu.VMEM((N_HEADS, HEAD_DIM, MOBA_BLOCK), F32),
                        pltpu.VMEM((N_HEADS, MOBA_BLOCK, MOBA_BLOCK), F32),
                        pltpu.VMEM((N_HEADS, MOBA_BLOCK, MOBA_BLOCK), F32)],
        compiler_params=_cparams(("arbitrary", "arbitrary")),
        name="moba_prompt",
    )(q, kmean, k_bf, vt_bf)


def _post_kernel(x_ref, od_ref, om_ref, ple_ref, wout_ref, fnorm_ref, wg_ref, wu_ref, wd_ref,
                 wple_ref, wpg_ref, onorm_ref, y_ref, *, ff_chunk):
    mix = jnp.concatenate([od_ref[...], om_ref[...]], axis=1).astype(BF16)
    h = x_ref[...] + _dot(mix, wout_ref[...])
    f = _rms(h, fnorm_ref[...]).astype(BF16)
    d_ff = wg_ref.shape[1]
    ffn = None
    for c0 in range(0, d_ff, ff_chunk):
        gate = _dot(f, wg_ref[:, c0:c0 + ff_chunk])
        up = _dot(f, wu_ref[:, c0:c0 + ff_chunk])
        part = _dot((_silu(gate) * up).astype(BF16), wd_ref[c0:c0 + ff_chunk, :])
        ffn = part if ffn is None else ffn + part
    h = h + ffn
    h = h + _dot(ple_ref[...].astype(BF16), wple_ref[...]) * _sigmoid(_dot(h.astype(BF16), wpg_ref[...]))
    y_ref[...] = _rms(h, onorm_ref[...])


def _post(x, o_delta, o_moba, ple, w_out, ffn_norm, w_gate, w_up, w_down, w_ple, w_ple_gate, out_norm, tm):
    t, d = x.shape
    d_ff = w_gate.shape[1]
    ff_chunk = d_ff // 2 if (d_ff // 2) % LANE == 0 else d_ff
    tok = lambda w: pl.BlockSpec((tm, w), lambda i: (i, 0))
    return pl.pallas_call(
        functools.partial(_post_kernel, ff_chunk=ff_chunk),
        grid=(t // tm,),
        in_specs=[tok(d), tok(GROUP_WIDTH), tok(GROUP_WIDTH), tok(ple.shape[1]),
                  _resident(w_out.shape), _resident((1, d)), _resident(w_gate.shape), _resident(w_up.shape),
                  _resident(w_down.shape), _resident(w_ple.shape), _resident(w_ple_gate.shape), _resident((1, d))],
        out_specs=tok(d),
        out_shape=jax.ShapeDtypeStruct((t, d), F32),
        compiler_params=_cparams(("arbitrary",)),
        name="post",
    )(x, o_delta, o_moba, ple, w_out, ffn_norm, w_gate, w_up, w_down, w_ple, w_ple_gate, out_norm)


def _delta_sample_kernel(xx_ref, gate_ref, z_ref, s_ref, convw_ref, alog_ref, dtb_ref, dnorm_ref,
                         o_ref, snew_ref, *, n_tok):
    nseq = xx_ref.shape[0]
    rows = nseq * SUBLANE
    xx = xx_ref[...].reshape(rows, CONV_CH)
    w = convw_ref[...]
    y = xx * w[0:1]
    for j in range(1, CONV_WIDTH):
        y = y + pltpu.roll(xx, rows - j, axis=0) * w[j:j + 1]
    y = _silu(y)
    gates = gate_ref[...].reshape(rows, LANE)
    real = (lax.broadcasted_iota(jnp.int32, (rows, 1), 0) % SUBLANE) < n_tok
    dnorm = dnorm_ref[...]
    for h in range(N_HEADS):
        q, k, v, g, beta = _delta_inputs(y, gates, alog_ref[...], dtb_ref[...], h)
        g = jnp.where(real, g, 0.0)
        beta = jnp.where(real, beta, 0.0)
        r3 = lambda t: t.reshape(nseq, SUBLANE, t.shape[-1])
        o, s_new = _chunk_apply(s_ref[:, h], *_chunk_prep(r3(q), r3(k), r3(v), r3(g), r3(beta), _bmm_exact))
        snew_ref[:, h] = s_new
        hs = slice(h * HEAD_DIM, (h + 1) * HEAD_DIM)
        o_ref[:, :, hs] = _rms(o, dnorm) * _silu(z_ref[:, :, hs])


def _delta_sample(xx, gates8, z8, state, conv_w, alog_row, dtb_row, dnorm, n_tok, nseq):
    nb = xx.shape[0]
    seq = lambda w: pl.BlockSpec((nseq, SUBLANE, w), lambda i: (i, 0, 0))
    st = pl.BlockSpec((nseq,) + state.shape[1:], lambda i: (i, 0, 0, 0))
    return pl.pallas_call(
        functools.partial(_delta_sample_kernel, n_tok=n_tok),
        grid=(nb // nseq,),
        in_specs=[seq(CONV_CH), seq(LANE), seq(GROUP_WIDTH), st, _resident(conv_w.shape),
                  _resident((1, LANE)), _resident((1, LANE)), _resident((1, HEAD_DIM))],
        out_specs=[seq(GROUP_WIDTH), st],
        out_shape=[jax.ShapeDtypeStruct((nb, SUBLANE, GROUP_WIDTH), F32), jax.ShapeDtypeStruct(state.shape, F32)],
        compiler_params=_cparams(("arbitrary",)),
        name="delta_sample",
    )(xx, gates8, z8, state, conv_w, alog_row, dtb_row, dnorm)


PAGES_PER_CHUNK = 16
N_SLOTS = 4


def _moba_sample_kernel(pt_ref, q_ref, kn_ref, vn_ref, kc_ref, vc_ref, o_ref,
                        buf, sem, s_scr, p_scr, psum, *, n_pages, page_size):
    b = pl.program_id(0)
    nseq = pl.num_programs(0)
    n_ck = n_pages // PAGES_PER_CHUNK
    n_stream = 2 * n_ck
    rows_per_page = page_size * N_HEADS
    cols = PAGES_PER_CHUNK * rows_per_page
    blk_cols = MOBA_BLOCK * N_HEADS
    n_blocks = n_pages * page_size // MOBA_BLOCK
    nq = q_ref.shape[1]
    scale = HEAD_DIM ** -0.5

    def copies(seq, j, slot):
        src = kc_ref if j < n_ck else vc_ref
        first = seq * n_pages + (j % n_ck) * PAGES_PER_CHUNK
        return [pltpu.make_async_copy(src.at[pt_ref[first + i]], buf.at[slot, i], sem.at[slot])
                for i in range(PAGES_PER_CHUNK)]

    def start_ahead(j):
        ahead = j + N_SLOTS - 1
        if ahead < n_stream:
            for cp in copies(b, ahead, ahead % N_SLOTS):
                cp.start()
        else:
            @pl.when(b + 1 < nseq)
            def _():
                for cp in copies(b + 1, ahead - n_stream, ahead % N_SLOTS):
                    cp.start()

    @pl.when(b == 0)
    def _():
        for j in range(N_SLOTS - 1):
            for cp in copies(0, j, j):
                cp.start()

    q = q_ref[0]
    q_bf = q.astype(BF16)

    for j in range(n_ck):
        slot = j % N_SLOTS
        for cp in copies(b, j, slot):
            cp.wait()
        start_ahead(j)
        pages = buf[slot]
        psum[j * PAGES_PER_CHUNK:(j + 1) * PAGES_PER_CHUNK] = jnp.sum(pages, axis=1)
        s_scr[:, j * cols:(j + 1) * cols] = _dot_nt(q_bf, pages.reshape(cols, HEAD_DIM).astype(BF16)) * scale

    prow = n_pages * 2 * N_HEADS
    gate_rows = _dot_nt(q, psum[...].reshape(prow, HEAD_DIM), HI)
    r_head = lax.broadcasted_iota(jnp.int32, (nq, prow), 0) % N_HEADS
    c_head = lax.broadcasted_iota(jnp.int32, (nq, prow), 1) % N_HEADS
    gate_rows = jnp.where(r_head == c_head, gate_rows, 0.0)
    rows_per_block = prow // n_blocks
    fold = jnp.where(lax.broadcasted_iota(jnp.int32, (prow, n_blocks), 0) // rows_per_block
                     == lax.broadcasted_iota(jnp.int32, (prow, n_blocks), 1), 1.0, 0.0)
    gate = _dot(gate_rows, fold, HI) * (1.0 / MOBA_BLOCK)
    sel = jnp.zeros((nq, n_blocks), F32)
    for hit in _topk_blocks(gate, jnp.full((nq, n_blocks), True), 1):
        sel = jnp.where(hit, 1.0, sel)

    head_ok = (lax.broadcasted_iota(jnp.int32, (nq, blk_cols), 0) % N_HEADS
               == lax.broadcasted_iota(jnp.int32, (nq, blk_cols), 1) % N_HEADS)
    orow = lax.broadcasted_iota(jnp.int32, (nq, nq), 0)
    ocol = lax.broadcasted_iota(jnp.int32, (nq, nq), 1)
    own_ok = (orow % N_HEADS == ocol % N_HEADS) & (ocol // N_HEADS <= orow // N_HEADS)
    s_own = jnp.where(own_ok, _dot_nt(q_bf, kn_ref[0].astype(BF16)) * scale, NEG)
    m = jnp.max(s_own, axis=1, keepdims=True)
    for n in range(n_blocks):
        cs = slice(n * blk_cols, (n + 1) * blk_cols)
        s_blk = jnp.where(head_ok & (sel[:, n:n + 1] > 0.5), s_scr[:, cs], NEG)
        s_scr[:, cs] = s_blk
        m = jnp.maximum(m, jnp.max(s_blk, axis=1, keepdims=True))
    p_own = jnp.exp(s_own - m)
    denom = jnp.sum(p_own, axis=1, keepdims=True)
    for n in range(n_blocks):
        cs = slice(n * blk_cols, (n + 1) * blk_cols)
        p_blk = jnp.exp(s_scr[:, cs] - m)
        denom = denom + jnp.sum(p_blk, axis=1, keepdims=True)
        p_scr[:, cs] = p_blk.astype(BF16)

    acc = _dot(p_own.astype(BF16), vn_ref[0].astype(BF16))
    for j in range(n_ck, n_stream):
        slot = j % N_SLOTS
        for cp in copies(b, j, slot):
            cp.wait()
        start_ahead(j)
        jc = j - n_ck
        acc = acc + _dot(p_scr[:, jc * cols:(jc + 1) * cols], buf[slot].reshape(cols, HEAD_DIM).astype(BF16))
    o_ref[0] = acc / denom


def _moba_sample(pt_flat, q16, kn16, vn16, cache_k4, cache_v4, n_pages):
    nb, nq, _ = q16.shape
    page_size = cache_k4.shape[1] * cache_k4.shape[2] // N_HEADS
    n_cols = n_pages * page_size * N_HEADS
    seq = pl.BlockSpec((1, nq, HEAD_DIM), lambda i, pt: (i, 0, 0))
    return pl.pallas_call(
        functools.partial(_moba_sample_kernel, n_pages=n_pages, page_size=page_size),
        grid_spec=pltpu.PrefetchScalarGridSpec(
            num_scalar_prefetch=1,
            grid=(nb,),
            in_specs=[seq, seq, seq, pl.BlockSpec(memory_space=pl.ANY), pl.BlockSpec(memory_space=pl.ANY)],
            out_specs=seq,
            scratch_shapes=[pltpu.VMEM((N_SLOTS, PAGES_PER_CHUNK) + cache_k4.shape[1:], F32),
                            pltpu.SemaphoreType.DMA((N_SLOTS,)),
                            pltpu.VMEM((nq, n_cols), F32),
                            pltpu.VMEM((nq, n_cols), BF16),
                            pltpu.VMEM((n_pages, 2 * N_HEADS, HEAD_DIM), F32)]),
        out_shape=jax.ShapeDtypeStruct((nb, nq, HEAD_DIM), F32),
        compiler_params=_cparams(("arbitrary",)),
        name="moba_sample",
    )(pt_flat, q16, kn16, vn16, cache_k4, cache_v4)


def _rope_tables(pos):
    half = HEAD_DIM // 2
    inv_freq = jnp.exp(jnp.arange(half, dtype=F32) * (-2.0 * math.log(ROPE_THETA) / HEAD_DIM))
    ang = pos.astype(F32)[:, None] * inv_freq[None, :]
    cos, sin = jnp.cos(ang), jnp.sin(ang)
    return jnp.concatenate([cos, cos], axis=-1), jnp.concatenate([-sin, sin], axis=-1)


def _pad_rows(a, n_tok):
    return jnp.pad(a, ((0, 0), (0, SUBLANE - n_tok), (0, 0)))


def _pick_tile(n, pref):
    t = min(pref, n)
    while n % t:
        t //= 2
    return t


def kernel(x_prompt, x_sample, cache_k, cache_v, page_table, state_delta, state_conv, p_prompt, p_sample,
           attn_norm, w_in, conv_w, a_log, dt_bias, delta_norm, w_out, ffn_norm, w_gate, w_up, w_down,
           w_ple, w_ple_gate, final_norm):
    bp, s_len, d = x_prompt.shape
    bs, n_tok, _ = x_sample.shape
    depth = w_in.shape[0]
    n_pages = page_table.shape[1]
    page_size = cache_k.shape[2]
    assert depth == 1, "intermediate layers would need the un-normalised hidden state"
    assert n_tok <= SUBLANE - (CONV_WIDTH - 1) and s_len % MOBA_BLOCK == 0 and s_len >= SUBLANE
    assert (n_pages * page_size) % MOBA_BLOCK == 0 and n_pages % PAGES_PER_CHUNK == 0 and page_size % 2 == 0
    assert (2 * n_pages // PAGES_PER_CHUNK) % N_SLOTS == 0
    assert n_pages * page_size // MOBA_BLOCK >= MOBA_TOPK and s_len // MOBA_BLOCK > MOBA_TOPK

    past = n_pages * page_size
    cos_p, sin_p = _rope_tables(jnp.arange(s_len, dtype=jnp.int32))
    cos_s, sin_s = _rope_tables(past + jnp.arange(n_tok, dtype=jnp.int32))
    cos_s, sin_s = jnp.tile(cos_s, (bs, 1)), jnp.tile(sin_s, (bs, 1))
    pt_flat = page_table.reshape(-1)
    row = lambda v: v.reshape(1, -1)

    hp = x_prompt.reshape(bp * s_len, d)
    hs = x_sample.reshape(bs * n_tok, d)
    tm_p = _pick_tile(s_len, 512)
    tm_s = _pick_tile(bs * n_tok, 128)
    tt = _pick_tile(s_len, SUBLANE * DELTA_CHUNK)
    outs = [[] for _ in range(8)]
    for i in range(depth):
        wi = w_in[i]
        c_dw = CONV_CH + GROUP_WIDTH
        w_in_r = jnp.concatenate(
            [wi[:, :c_dw], wi[:, c_dw + 2 * N_HEADS:], wi[:, c_dw:c_dw + 2 * N_HEADS],
             jnp.zeros((d, LANE - 2 * N_HEADS), F32)], axis=1).astype(BF16)
        alog_row = jnp.zeros((1, LANE), F32).at[0, N_HEADS:2 * N_HEADS].set(a_log[i])
        dtb_row = jnp.zeros((1, LANE), F32).at[0, N_HEADS:2 * N_HEADS].set(dt_bias[i])
        wb = lambda w: w[i].astype(BF16)
        post_w = (wb(w_out), row(ffn_norm[i]), wb(w_gate), wb(w_up), wb(w_down), wb(w_ple), wb(w_ple_gate))
        out_norm = row(final_norm)

        qkv, z, q_m, k_m, v_m, gates, k_bf, vt_bf, ksum = _in_proj(hp, row(attn_norm[i]), w_in_r, cos_p, sin_p, tm_p)
        sh = lambda a: a.reshape(bp, s_len, a.shape[-1])
        qkv3 = sh(qkv)
        hist0 = jnp.zeros((bp, CONV_WIDTH - 1, CONV_CH), F32)
        u, wk, qd, kd, qk, cd = _delta_prep(qkv3, sh(gates), hist0, conv_w[i], alog_row, dtb_row, tt)
        s00 = jnp.zeros((bp, N_HEADS, HEAD_DIM, HEAD_DIM), F32)
        o_delta, d_p = _delta_scan(u, wk, qd, kd, qk, cd, sh(z), s00, row(delta_norm[i]), tt)
        nblk = s_len // MOBA_BLOCK
        kmean = ksum.reshape(bp, s_len // tm_p, -1, GROUP_WIDTH)
        kmean = kmean.reshape(bp, -1, GROUP_WIDTH)
        if tm_p < MOBA_BLOCK:
            kmean = kmean.reshape(bp, nblk, MOBA_BLOCK // tm_p, GROUP_WIDTH).sum(axis=2)
        kmean = kmean * (1.0 / MOBA_BLOCK)
        o_moba = _moba_prompt(sh(q_m), kmean, sh(k_bf), vt_bf)
        hp_new = _post(hp, o_delta.reshape(bp * s_len, GROUP_WIDTH), o_moba.reshape(bp * s_len, GROUP_WIDTH),
                       p_prompt[i].reshape(bp * s_len, -1), *post_w, out_norm, tm_p)
        outs[0].append(k_m.reshape(bp, s_len, N_HEADS, HEAD_DIM))
        outs[1].append(v_m.reshape(bp, s_len, N_HEADS, HEAD_DIM))
        outs[4].append(d_p)
        outs[6].append(qkv3[:, s_len - (CONV_WIDTH - 1):])
        hp = hp_new

        qkv, z, q_m, k_m, v_m, gates, _, _, _ = _in_proj(hs, row(attn_norm[i]), w_in_r, cos_s, sin_s, tm_s)
        sh = lambda a: a.reshape(bs, n_tok, a.shape[-1])
        qkv3 = sh(qkv)
        xx_full = jnp.concatenate([state_conv[i], qkv3], axis=1)
        xx = jnp.pad(xx_full, ((0, 0), (0, SUBLANE - xx_full.shape[1]), (0, 0)))
        nseq = _pick_tile(bs, 8)
        o_delta8, d_s = _delta_sample(xx, _pad_rows(sh(gates), n_tok), _pad_rows(sh(z), n_tok), state_delta[i],
                                      conv_w[i], alog_row, dtb_row, row(delta_norm[i]), n_tok, nseq)
        n_pool = cache_k.shape[1]
        page_view = (n_pool, page_size // 2, 2 * N_HEADS, HEAD_DIM)
        rows = lambda a: a.reshape(bs, n_tok * N_HEADS, HEAD_DIM)
        o_moba_s = _moba_sample(pt_flat, rows(q_m), rows(k_m), rows(v_m),
                                cache_k[i].reshape(page_view), cache_v[i].reshape(page_view), n_pages)
        hs_new = _post(hs, o_delta8[:, :n_tok].reshape(bs * n_tok, GROUP_WIDTH),
                       o_moba_s.reshape(bs * n_tok, GROUP_WIDTH),
                       p_sample[i].reshape(bs * n_tok, -1), *post_w, out_norm, tm_s)
        outs[2].append(k_m.reshape(bs, n_tok, N_HEADS, HEAD_DIM))
        outs[3].append(v_m.reshape(bs, n_tok, N_HEADS, HEAD_DIM))
        outs[5].append(d_s)
        outs[7].append(xx_full[:, xx_full.shape[1] - (CONV_WIDTH - 1):])
        hs = hs_new

    y_prompt = hp.reshape(bp, s_len, d)
    y_sample = hs.reshape(bs, n_tok, d)
    st = [jnp.stack(o) for o in outs]
    return (y_prompt, y_sample, st[0], st[1], st[2], st[3], st[4], st[5], st[6], st[7])
```

```python
import functools
import math

import jax
import jax.numpy as jnp
from jax import lax
from jax.experimental import pallas as pl
from jax.experimental.pallas import tpu as pltpu

F32 = jnp.float32
BF16 = jnp.bfloat16
HI = lax.Precision.HIGHEST

HEAD_DIM = 128
N_HEADS = 4
GROUP_WIDTH = N_HEADS * HEAD_DIM
CONV_WIDTH = 4
CONV_CH = 3 * GROUP_WIDTH
DELTA_CHUNK = 64
MOBA_BLOCK = 256
MOBA_TOPK = 3
ROPE_THETA = 10000.0
RMS_EPS = 1e-6
L2_EPS = 1e-6
NEG = -1e30
LANE = 128
SUBLANE = 8
VMEM_LIMIT = 56 * 1024 * 1024

C_QKV = 0
C_Z = CONV_CH
C_QM = C_Z + GROUP_WIDTH
C_KM = C_QM + GROUP_WIDTH
C_VM = C_KM + GROUP_WIDTH
C_GATE = C_VM + GROUP_WIDTH
IN_COLS = C_GATE + LANE


def _cparams(sem):
    return pltpu.CompilerParams(dimension_semantics=sem, vmem_limit_bytes=VMEM_LIMIT)


def _resident(shape):
    nd = len(shape)
    return pl.BlockSpec(shape, lambda *_: (0,) * nd, pipeline_mode=pl.Buffered(1))


def _sigmoid(x):
    return 1.0 / (1.0 + jnp.exp(-x))


def _silu(x):
    return x * _sigmoid(x)


def _softplus(x):
    return jnp.maximum(x, 0.0) + jnp.log(1.0 + jnp.exp(-jnp.abs(x)))


def _rms(x, g):
    return x * lax.rsqrt(jnp.mean(x * x, axis=-1, keepdims=True) + RMS_EPS) * g


def _dot(a, b, prec=None):
    return jnp.dot(a, b, precision=prec, preferred_element_type=F32)


def _dot_nt(a, b, prec=None):
    return lax.dot_general(a, b, (((1,), (1,)), ((), ())), precision=prec, preferred_element_type=F32)


def _bmm(a, b, prec=None):
    return lax.dot_general(a, b, (((2,), (1,)), ((0,), (0,))), precision=prec, preferred_element_type=F32)


def _bmm_nt(a, b, prec=None):
    return lax.dot_general(a, b, (((2,), (2,)), ((0,), (0,))), precision=prec, preferred_element_type=F32)


def _bmm_tn(a, b, prec=None):
    return lax.dot_general(a, b, (((1,), (1,)), ((0,), (0,))), precision=prec, preferred_element_type=F32)


def _in_proj_kernel(x_ref, g_ref, w_ref, cos_ref, sin_ref,
                    qkv_ref, z_ref, q_ref, k_ref, v_ref, gate_ref, kbf_ref, vt_ref, ksum_ref):
    a = _rms(x_ref[...], g_ref[...]).astype(BF16)
    cos = cos_ref[...]
    sin = sin_ref[...]

    def proj(c0, width):
        return _dot(a, w_ref[:, c0:c0 + width])

    def rope(t):
        heads = []
        for h in range(N_HEADS):
            th = t[:, h * HEAD_DIM:(h + 1) * HEAD_DIM]
            heads.append(th * cos + pltpu.roll(th, HEAD_DIM // 2, axis=1) * sin)
        return jnp.concatenate(heads, axis=1)

    qkv_ref[...] = proj(C_QKV, CONV_CH)
    z_ref[...] = proj(C_Z, GROUP_WIDTH)
    q_ref[...] = rope(proj(C_QM, GROUP_WIDTH))
    k = rope(proj(C_KM, GROUP_WIDTH))
    v = proj(C_VM, GROUP_WIDTH)
    k_ref[...] = k
    v_ref[...] = v
    kbf_ref[...] = k.astype(BF16)
    vt_ref[0] = v.T.astype(BF16)
    nblk = ksum_ref.shape[1]
    ksum_ref[0] = jnp.sum(k.reshape(nblk, k.shape[0] // nblk, GROUP_WIDTH), axis=1)
    gate_ref[...] = proj(C_GATE, LANE)


def _in_proj(x, norm_w, w_in_r, cos, sin, tm):
    t, d = x.shape
    nt = t // tm
    tiles_per_seq = cos.shape[0] // tm
    n_tab = tiles_per_seq
    nblk = max(tm // MOBA_BLOCK, 1)
    tok = lambda w: pl.BlockSpec((tm, w), lambda i: (i, 0))
    tab = pl.BlockSpec((tm, HEAD_DIM), lambda i: (i % n_tab, 0))
    out_shapes = [
        jax.ShapeDtypeStruct((t, CONV_CH), F32), jax.ShapeDtypeStruct((t, GROUP_WIDTH), F32),
        jax.ShapeDtypeStruct((t, GROUP_WIDTH), F32), jax.ShapeDtypeStruct((t, GROUP_WIDTH), F32),
        jax.ShapeDtypeStruct((t, GROUP_WIDTH), F32), jax.ShapeDtypeStruct((t, LANE), F32),
        jax.ShapeDtypeStruct((t, GROUP_WIDTH), BF16),
        jax.ShapeDtypeStruct((nt // tiles_per_seq, GROUP_WIDTH, tiles_per_seq * tm), BF16),
        jax.ShapeDtypeStruct((nt, nblk, GROUP_WIDTH), F32),
    ]
    out_specs = [tok(CONV_CH), tok(GROUP_WIDTH), tok(GROUP_WIDTH), tok(GROUP_WIDTH), tok(GROUP_WIDTH),
                 tok(LANE), tok(GROUP_WIDTH),
                 pl.BlockSpec((1, GROUP_WIDTH, tm), lambda i: (i // tiles_per_seq, 0, i % tiles_per_seq)),
                 pl.BlockSpec((1, nblk, GROUP_WIDTH), lambda i: (i, 0, 0))]
    return pl.pallas_call(
        _in_proj_kernel,
        grid=(nt,),
        in_specs=[tok(d), _resident((1, d)), _resident(w_in_r.shape), tab, tab],
        out_specs=out_specs,
        out_shape=out_shapes,
        compiler_params=_cparams(("arbitrary",)),
        name="in_proj",
    )(x, norm_w, w_in_r, cos, sin)


def _split(x):
    hi = x.astype(BF16)
    return hi, (x - hi.astype(F32)).astype(BF16)


def _bmm_split(a, b, nt=False):
    ah, al = _split(a)
    bh, bl = _split(b)
    lhs = jnp.concatenate([ah, al, ah], axis=2)
    if nt:
        return _bmm_nt(lhs, jnp.concatenate([bh, bh, bl], axis=2))
    return _bmm(lhs, jnp.concatenate([bh, bh, bl], axis=1))


def _bmm_exact(a, b, nt=False):
    return _bmm_nt(a, b, HI) if nt else _bmm(a, b, HI)


def _chunk_prep(q, k, v, g, beta, mm):
    n, c, dh = q.shape
    row = lax.broadcasted_iota(jnp.int32, (c, c), 0)
    col = lax.broadcasted_iota(jnp.int32, (c, c), 1)
    causal = row >= col
    strict = row > col
    gx = jnp.concatenate([jnp.broadcast_to(g, (n, c, dh)),
                          jnp.where(strict, jnp.broadcast_to(g, (n, c, c)), 0.0)], axis=2)
    g1 = gx.astype(BF16)
    r1 = gx - g1.astype(F32)
    g2 = r1.astype(BF16)
    g3 = (r1 - g2.astype(F32)).astype(BF16)
    lower3 = jnp.broadcast_to(jnp.concatenate([causal.astype(BF16)] * 3, axis=1), (n, c, 3 * c))
    csum = _bmm(lower3, jnp.concatenate([g1, g2, g3], axis=1))
    gcb = csum[:, :, :dh]
    dlog = csum[:, :, dh:]
    decay = jnp.where(causal, jnp.exp(jnp.where(causal, dlog, 0.0)), 0.0)
    kb = k * beta
    kq = mm(jnp.concatenate([kb, q], axis=1), k, nt=True)
    a_mat = jnp.where(strict, kq[:, :c] * decay, 0.0)
    qk = jnp.where(causal, kq[:, c:] * decay, 0.0)
    nmat = -a_mat
    t_inv = jnp.where(row == col, 1.0, 0.0) + nmat
    for _ in range(int(math.log2(c)) - 1):
        nmat = mm(nmat, nmat)
        t_inv = t_inv + mm(t_inv, nmat)
    egc = jnp.exp(gcb)
    uw = mm(t_inv, jnp.concatenate([v * beta, kb * egc], axis=2))
    g_last = gcb[:, c - 1:c, :]
    return uw[:, :, :dh], uw[:, :, dh:], qk, q * egc, k * jnp.exp(g_last - gcb), jnp.exp(g_last)


def _chunk_apply(s, u, wk, qk, qd, kd, cd):
    c = u.shape[1]
    xs = _bmm(jnp.concatenate([wk, qd], axis=1).astype(BF16), s.astype(BF16))
    w = u - xs[:, :c]
    wb = w.astype(BF16)
    o = xs[:, c:] + _bmm(qk.astype(BF16), wb)
    s_new = s * cd + _bmm_tn(kd.astype(BF16), wb)
    return o, s_new


def _delta_inputs(y, gates, alog, dtb, h):
    sl = lambda base: slice(base + h * HEAD_DIM, base + (h + 1) * HEAD_DIM)
    q = y[:, sl(0)]
    k = y[:, sl(GROUP_WIDTH)]
    v = y[:, sl(2 * GROUP_WIDTH)]
    q = q * lax.rsqrt(jnp.sum(q * q, axis=-1, keepdims=True) + L2_EPS) * (HEAD_DIM ** -0.5)
    k = k * lax.rsqrt(jnp.sum(k * k, axis=-1, keepdims=True) + L2_EPS)
    beta = _sigmoid(gates[:, h:h + 1])
    g = -jnp.exp(alog[:, N_HEADS + h:N_HEADS + h + 1]) * _softplus(
        gates[:, N_HEADS + h:N_HEADS + h + 1] + dtb[:, N_HEADS + h:N_HEADS + h + 1])
    return q, k, v, g, beta


def _delta_prep_kernel(qkv_ref, gate_ref, hist_ref, convw_ref, alog_ref, dtb_ref,
                       u_ref, wk_ref, qd_ref, kd_ref, qk_ref, cd_ref, prev_ref):
    tt = qkv_ref.shape[1]
    n = tt // DELTA_CHUNK

    @pl.when(pl.program_id(1) == 0)
    def _():
        prev_ref[...] = jnp.zeros_like(prev_ref)
        prev_ref[SUBLANE - (CONV_WIDTH - 1):, :] = hist_ref[0]

    x = qkv_ref[0]
    full = jnp.concatenate([prev_ref[...], x], axis=0)
    w = convw_ref[...]
    base = SUBLANE - (CONV_WIDTH - 1)
    y = full[base:base + tt] * w[0:1]
    for j in range(1, CONV_WIDTH):
        y = y + full[base + j:base + j + tt] * w[j:j + 1]
    y = _silu(y)
    prev_ref[...] = x[tt - SUBLANE:]

    gates = gate_ref[0]
    alog = alog_ref[...]
    dtb = dtb_ref[...]
    for h in range(N_HEADS):
        q, k, v, g, beta = _delta_inputs(y, gates, alog, dtb, h)
        r3 = lambda t: t.reshape(n, DELTA_CHUNK, t.shape[-1])
        u, wk, qk, qd, kd, cd = _chunk_prep(r3(q), r3(k), r3(v), r3(g), r3(beta), _bmm_split)
        hs = slice(h * HEAD_DIM, (h + 1) * HEAD_DIM)
        u_ref[0, :, hs] = u.reshape(tt, HEAD_DIM)
        wk_ref[0, :, hs] = wk.reshape(tt, HEAD_DIM)
        qd_ref[0, :, hs] = qd.reshape(tt, HEAD_DIM)
        kd_ref[0, :, hs] = kd.reshape(tt, HEAD_DIM)
        qk_ref[0, :, h * DELTA_CHUNK:(h + 1) * DELTA_CHUNK] = qk.reshape(tt, DELTA_CHUNK)
        cd_ref[0, :, hs] = cd.reshape(n, HEAD_DIM)


def _delta_prep(qkv, gates, hist, conv_w, alog_row, dtb_row, tt):
    b, s, _ = qkv.shape
    n = tt // DELTA_CHUNK
    tok = lambda w: pl.BlockSpec((1, tt, w), lambda bi, j: (bi, j, 0))
    wide = jax.ShapeDtypeStruct((b, s, GROUP_WIDTH), F32)
    return pl.pallas_call(
        _delta_prep_kernel,
        grid=(b, s // tt),
        in_specs=[tok(CONV_CH), tok(LANE),
                  pl.BlockSpec((1, CONV_WIDTH - 1, CONV_CH), lambda bi, j: (bi, 0, 0)),
                  _resident(conv_w.shape), _resident((1, LANE)), _resident((1, LANE))],
        out_specs=[tok(GROUP_WIDTH), tok(GROUP_WIDTH), tok(GROUP_WIDTH), tok(GROUP_WIDTH),
                   tok(N_HEADS * DELTA_CHUNK),
                   pl.BlockSpec((1, n, GROUP_WIDTH), lambda bi, j: (bi, j, 0))],
        out_shape=[wide, wide, wide, wide,
                   jax.ShapeDtypeStruct((b, s, N_HEADS * DELTA_CHUNK), F32),
                   jax.ShapeDtypeStruct((b, s // DELTA_CHUNK, GROUP_WIDTH), F32)],
        scratch_shapes=[pltpu.VMEM((SUBLANE, CONV_CH), F32)],
        compiler_params=_cparams(("arbitrary", "arbitrary")),
        name="delta_prep",
    )(qkv, gates, hist, conv_w, alog_row, dtb_row)


def _delta_scan_kernel(u_ref, wk_ref, qd_ref, kd_ref, qk_ref, cd_ref, z_ref, s0_ref, dnorm_ref,
                       o_ref, sfin_ref, s_scr):
    nb, tt, _ = u_ref.shape
    n = tt // DELTA_CHUNK

    @pl.when(pl.program_id(0) == 0)
    def _():
        s_scr[...] = s0_ref[...]

    dnorm = dnorm_ref[...]
    for ci in range(n):
        rows = slice(ci * DELTA_CHUNK, (ci + 1) * DELTA_CHUNK)
        for b in range(nb):
            for h in range(N_HEADS):
                hs = slice(h * HEAD_DIM, (h + 1) * HEAD_DIM)
                o, s_new = _chunk_apply(
                    s_scr[b, h][None], u_ref[b, rows, hs][None], wk_ref[b, rows, hs][None],
                    qk_ref[b, rows, h * DELTA_CHUNK:(h + 1) * DELTA_CHUNK][None],
                    qd_ref[b, rows, hs][None], kd_ref[b, rows, hs][None], cd_ref[b, ci:ci + 1, hs][None])
                s_scr[b, h] = s_new[0]
                o_ref[b, rows, hs] = _rms(o[0], dnorm) * _silu(z_ref[b, rows, hs])

    @pl.when(pl.program_id(0) == pl.num_programs(0) - 1)
    def _():
        sfin_ref[...] = s_scr[...]


def _delta_scan(u, wk, qd, kd, qk, cd, z, s0, dnorm, tt):
    b, s, _ = u.shape
    n = tt // DELTA_CHUNK
    tok = lambda w: pl.BlockSpec((b, tt, w), lambda j: (0, j, 0))
    st = pl.BlockSpec(s0.shape, lambda j: (0, 0, 0, 0))
    return pl.pallas_call(
        _delta_scan_kernel,
        grid=(s // tt,),
        in_specs=[tok(GROUP_WIDTH), tok(GROUP_WIDTH), tok(GROUP_WIDTH), tok(GROUP_WIDTH),
                  tok(N_HEADS * DELTA_CHUNK), pl.BlockSpec((b, n, GROUP_WIDTH), lambda j: (0, j, 0)),
                  tok(GROUP_WIDTH), st, _resident((1, HEAD_DIM))],
        out_specs=[tok(GROUP_WIDTH), st],
        out_shape=[jax.ShapeDtypeStruct((b, s, GROUP_WIDTH), F32), jax.ShapeDtypeStruct(s0.shape, F32)],
        scratch_shapes=[pltpu.VMEM(s0.shape, F32)],
        compiler_params=_cparams(("arbitrary",)),
        name="delta_scan",
    )(u, wk, qd, kd, qk, cd, z, s0, dnorm)


def _topk_blocks(gate, valid, axis):
    nb = gate.shape[axis]
    blk = lax.broadcasted_iota(jnp.int32, gate.shape, axis).astype(F32)
    gv = jnp.where(valid, gate, -jnp.inf)
    hits = []
    for _ in range(MOBA_TOPK):
        top = jnp.max(gv, axis=axis, keepdims=True)
        first = jnp.min(jnp.where(gv == top, blk, float(nb)), axis=axis, keepdims=True)
        hit = (blk == first) & (top > -jnp.inf)
        gv = jnp.where(hit, -jnp.inf, gv)
        hits.append(hit)
    return hits


def _moba_prompt_kernel(q_ref, kmean_ref, k_ref, vt_ref, o_ref,
                        bias_scr, qbf_scr, m_scr, l_scr, acc_scr, sa_scr, sb_scr):
    cur = pl.program_id(1)
    tq = q_ref.shape[1]
    nb = kmean_ref.shape[1]
    scale = HEAD_DIM ** -0.5
    key_i = lax.broadcasted_iota(jnp.int32, (MOBA_BLOCK, tq), 0)
    qry_i = lax.broadcasted_iota(jnp.int32, (MOBA_BLOCK, tq), 1)
    causal_bias = jnp.where(key_i <= qry_i, 0.0, NEG)
    blk_ids = lax.broadcasted_iota(jnp.int32, (nb, tq), 0)
    heads = [slice(h * HEAD_DIM, (h + 1) * HEAD_DIM) for h in range(N_HEADS)]

    for h, hs in enumerate(heads):
        q = q_ref[0, :, hs]
        gate_t = _dot_nt(kmean_ref[0, :, hs], q, HI)
        chosen = jnp.zeros((nb, tq), jnp.bool_)
        for hit in _topk_blocks(gate_t, blk_ids < cur, 0):
            chosen = chosen | hit
        bias_scr[h] = jnp.where(chosen, 0.0, NEG)
        qbf_scr[h] = (q * scale).astype(BF16)
        m_scr[h] = jnp.full((1, tq), NEG, F32)
        l_scr[h] = jnp.zeros((1, tq), F32)
        acc_scr[h] = jnp.zeros((HEAD_DIM, tq), F32)

    def scores(jb, s_buf):
        r0 = pl.multiple_of(jb * MOBA_BLOCK, MOBA_BLOCK)
        for h, hs in enumerate(heads):
            s_buf[h] = _dot_nt(k_ref[0, pl.ds(r0, MOBA_BLOCK), hs], qbf_scr[h])

    def update(jb, s_buf, bias_of):
        r0 = pl.multiple_of(jb * MOBA_BLOCK, MOBA_BLOCK)
        for h, hs in enumerate(heads):
            s = s_buf[h] + bias_of(h)
            m_i = m_scr[h]
            m_new = jnp.maximum(m_i, jnp.max(s, axis=0, keepdims=True))
            alpha = jnp.exp(m_i - m_new)
            p = jnp.exp(s - m_new)
            m_scr[h] = m_new
            l_scr[h] = alpha * l_scr[h] + jnp.sum(p, axis=0, keepdims=True)
            acc_scr[h] = alpha * acc_scr[h] + _dot(vt_ref[0, hs, pl.ds(r0, MOBA_BLOCK)], p.astype(BF16))

    def chosen_bias(jb):
        return lambda h: bias_scr[h, pl.ds(jb, 1), :]

    def own_bias(h):
        return causal_bias

    scores(0, sa_scr)

    def block_pair(pair, carry):
        jb = 2 * pair
        scores(jb + 1, sb_scr)
        update(jb, sa_scr, chosen_bias(jb))
        scores(jb + 2, sa_scr)
        update(jb + 1, sb_scr, chosen_bias(jb + 1))
        return carry

    lax.fori_loop(0, cur // 2, block_pair, 0)

    @pl.when(cur % 2 == 1)
    def _():
        scores(cur, sb_scr)
        update(cur - 1, sa_scr, chosen_bias(cur - 1))
        update(cur, sb_scr, own_bias)

    @pl.when(cur % 2 == 0)
    def _():
        update(cur, sa_scr, own_bias)

    for h, hs in enumerate(heads):
        o_ref[0, :, hs] = (acc_scr[h] / l_scr[h]).T


def _moba_prompt(q, kmean, k_bf, vt_bf):
    b, s, _ = q.shape
    nb = s // MOBA_BLOCK
    tok = pl.BlockSpec((1, MOBA_BLOCK, GROUP_WIDTH), lambda bi, j: (bi, j, 0))
    seq = lambda a: pl.BlockSpec((1,) + a.shape[1:], lambda bi, j: (bi, 0, 0))
    return pl.pallas_call(
        _moba_prompt_kernel,
        grid=(b, nb),
        in_specs=[tok, seq(kmean), seq(k_bf), seq(vt_bf)],
        out_specs=tok,
        out_shape=jax.ShapeDtypeStruct((b, s, GROUP_WIDTH), F32),
        scratch_shapes=[pltpu.VMEM((N_HEADS, nb, MOBA_BLOCK), F32),
                        pltpu.VMEM((N_HEADS, MOBA_BLOCK, HEAD_DIM), BF16),
                        pltpu.VMEM((N_HEADS, 1, MOBA_BLOCK), F32),
                        pltpu.VMEM((N_HEADS, 1, MOBA_BLOCK), F32),
                        pltpu.VMEM((N_HEADS, HEAD_DIM, MOBA_BLOCK), F32),
                        pltpu.VMEM((N_HEADS, MOBA_BLOCK, MOBA_BLOCK), F32),
                        pltpu.VMEM((N_HEADS, MOBA_BLOCK, MOBA_BLOCK), F32)],
        compiler_params=_cparams(("arbitrary", "arbitrary")),
        name="moba_prompt",
    )(q, kmean, k_bf, vt_bf)


FF_CHUNK = 256
N_POST_STAGES = 8


def _post_stages(x_ref, od_ref, om_ref, ple_ref, wout_ref, fnorm_ref, wg_ref, wu_ref, wd_ref,
                 wple_ref, wpg_ref, onorm_ref, y_ref, h_scr, f_scr, ffn_scr):
    d_ff = wg_ref.shape[1]
    ff = FF_CHUNK if d_ff % FF_CHUNK == 0 else d_ff
    starts = list(range(0, d_ff, ff))
    n_mid = N_POST_STAGES - 2
    groups = [starts[i * len(starts) // n_mid:(i + 1) * len(starts) // n_mid] for i in range(n_mid)]

    def head():
        mix = jnp.concatenate([od_ref[...], om_ref[...]], axis=1).astype(BF16)
        h = x_ref[...] + _dot(mix, wout_ref[...])
        h_scr[...] = h
        f_scr[...] = _rms(h, fnorm_ref[...]).astype(BF16)
        ffn_scr[...] = jnp.zeros_like(ffn_scr)

    def ffn(group):
        def run():
            f = f_scr[...]
            part = None
            for c0 in group:
                gate = _dot(f, wg_ref[:, c0:c0 + ff])
                up = _dot(f, wu_ref[:, c0:c0 + ff])
                down = _dot((_silu(gate) * up).astype(BF16), wd_ref[c0:c0 + ff, :])
                part = down if part is None else part + down
            if part is not None:
                ffn_scr[...] += part
        return run

    def tail():
        h = h_scr[...] + ffn_scr[...]
        h = h + _dot(ple_ref[...].astype(BF16), wple_ref[...]) * _sigmoid(_dot(h.astype(BF16), wpg_ref[...]))
        y_ref[...] = _rms(h, onorm_ref[...])

    return [head] + [ffn(g) for g in groups] + [tail]


def _post_scratch(tm, d):
    return [pltpu.VMEM((tm, d), F32), pltpu.VMEM((tm, d), BF16), pltpu.VMEM((tm, d), F32)]


def _post_kernel(*refs):
    for stage in _post_stages(*refs):
        stage()


def _post(x, o_delta, o_moba, ple, w_out, ffn_norm, w_gate, w_up, w_down, w_ple, w_ple_gate, out_norm, tm):
    t, d = x.shape
    tok = lambda w: pl.BlockSpec((tm, w), lambda i: (i, 0))
    return pl.pallas_call(
        _post_kernel,
        grid=(t // tm,),
        in_specs=[tok(d), tok(GROUP_WIDTH), tok(GROUP_WIDTH), tok(ple.shape[1]),
                  _resident(w_out.shape), _resident((1, d)), _resident(w_gate.shape), _resident(w_up.shape),
                  _resident(w_down.shape), _resident(w_ple.shape), _resident(w_ple_gate.shape), _resident((1, d))],
        out_specs=tok(d),
        out_shape=jax.ShapeDtypeStruct((t, d), F32),
        scratch_shapes=_post_scratch(tm, d),
        compiler_params=_cparams(("arbitrary",)),
        name="post",
    )(x, o_delta, o_moba, ple, w_out, ffn_norm, w_gate, w_up, w_down, w_ple, w_ple_gate, out_norm)


def _delta_sample_kernel(xx_ref, gate_ref, z_ref, s_ref, convw_ref, alog_ref, dtb_ref, dnorm_ref,
                         o_ref, snew_ref, *, n_tok):
    nseq = xx_ref.shape[0]
    rows = nseq * SUBLANE
    xx = xx_ref[...].reshape(rows, CONV_CH)
    w = convw_ref[...]
    y = xx * w[0:1]
    for j in range(1, CONV_WIDTH):
        y = y + pltpu.roll(xx, rows - j, axis=0) * w[j:j + 1]
    y = _silu(y)
    gates = gate_ref[...].reshape(rows, LANE)
    real = (lax.broadcasted_iota(jnp.int32, (rows, 1), 0) % SUBLANE) < n_tok
    dnorm = dnorm_ref[...]
    for h in range(N_HEADS):
        q, k, v, g, beta = _delta_inputs(y, gates, alog_ref[...], dtb_ref[...], h)
        g = jnp.where(real, g, 0.0)
        beta = jnp.where(real, beta, 0.0)
        r3 = lambda t: t.reshape(nseq, SUBLANE, t.shape[-1])
        o, s_new = _chunk_apply(s_ref[:, h], *_chunk_prep(r3(q), r3(k), r3(v), r3(g), r3(beta), _bmm_exact))
        snew_ref[:, h] = s_new
        hs = slice(h * HEAD_DIM, (h + 1) * HEAD_DIM)
        o_ref[:, :, hs] = _rms(o, dnorm) * _silu(z_ref[:, :, hs])


def _delta_sample(xx, gates8, z8, state, conv_w, alog_row, dtb_row, dnorm, n_tok, nseq):
    nb = xx.shape[0]
    seq = lambda w: pl.BlockSpec((nseq, SUBLANE, w), lambda i: (i, 0, 0))
    st = pl.BlockSpec((nseq,) + state.shape[1:], lambda i: (i, 0, 0, 0))
    return pl.pallas_call(
        functools.partial(_delta_sample_kernel, n_tok=n_tok),
        grid=(nb // nseq,),
        in_specs=[seq(CONV_CH), seq(LANE), seq(GROUP_WIDTH), st, _resident(conv_w.shape),
                  _resident((1, LANE)), _resident((1, LANE)), _resident((1, HEAD_DIM))],
        out_specs=[seq(GROUP_WIDTH), st],
        out_shape=[jax.ShapeDtypeStruct((nb, SUBLANE, GROUP_WIDTH), F32), jax.ShapeDtypeStruct(state.shape, F32)],
        compiler_params=_cparams(("arbitrary",)),
        name="delta_sample",
    )(xx, gates8, z8, state, conv_w, alog_row, dtb_row, dnorm)


PAGES_PER_CHUNK = 16
N_SLOTS = 4


def _moba_sample_kernel(pt_ref, q_ref, kn_ref, vn_ref, kc_ref, vc_ref, *rest, n_pages, page_size):
    post_in, (o_ref, y_ref), (buf, sem, s_scr, p_scr, psum, h_scr, f_scr, ffn_scr) = rest[:12], rest[12:14], rest[14:]
    stages = _post_stages(*post_in, y_ref, h_scr, f_scr, ffn_scr)
    b = pl.program_id(0)
    nseq = pl.num_programs(0)
    n_ck = n_pages // PAGES_PER_CHUNK
    n_stream = 2 * n_ck
    rows_per_page = page_size * N_HEADS
    cols = PAGES_PER_CHUNK * rows_per_page
    blk_cols = MOBA_BLOCK * N_HEADS
    n_blocks = n_pages * page_size // MOBA_BLOCK
    nq = q_ref.shape[1]
    scale = HEAD_DIM ** -0.5

    def copies(seq, j, slot):
        src = kc_ref if j < n_ck else vc_ref
        first = seq * n_pages + (j % n_ck) * PAGES_PER_CHUNK
        return [pltpu.make_async_copy(src.at[pt_ref[first + i]], buf.at[slot, i], sem.at[slot])
                for i in range(PAGES_PER_CHUNK)]

    def start_ahead(j):
        ahead = j + N_SLOTS - 1
        if ahead < n_stream:
            for cp in copies(b, ahead, ahead % N_SLOTS):
                cp.start()
        else:
            @pl.when(b + 1 < nseq)
            def _():
                for cp in copies(b + 1, ahead - n_stream, ahead % N_SLOTS):
                    cp.start()

    @pl.when(b == 0)
    def _():
        for j in range(N_SLOTS - 1):
            for cp in copies(0, j, j):
                cp.start()

    def run_stages(j):
        for stage in stages[j * len(stages) // n_stream:(j + 1) * len(stages) // n_stream]:
            stage()

    q = q_ref[0]
    q_bf = q.astype(BF16)

    for j in range(n_ck):
        slot = j % N_SLOTS
        for cp in copies(b, j, slot):
            cp.wait()
        start_ahead(j)
        pages = buf[slot]
        psum[j * PAGES_PER_CHUNK:(j + 1) * PAGES_PER_CHUNK] = jnp.sum(pages, axis=1)
        s_scr[:, j * cols:(j + 1) * cols] = _dot_nt(q_bf, pages.reshape(cols, HEAD_DIM).astype(BF16)) * scale
        run_stages(j)

    prow = n_pages * 2 * N_HEADS
    gate_rows = _dot_nt(q, psum[...].reshape(prow, HEAD_DIM), HI)
    r_head = lax.broadcasted_iota(jnp.int32, (nq, prow), 0) % N_HEADS
    c_head = lax.broadcasted_iota(jnp.int32, (nq, prow), 1) % N_HEADS
    gate_rows = jnp.where(r_head == c_head, gate_rows, 0.0)
    rows_per_block = prow // n_blocks
    fold = jnp.where(lax.broadcasted_iota(jnp.int32, (prow, n_blocks), 0) // rows_per_block
                     == lax.broadcasted_iota(jnp.int32, (prow, n_blocks), 1), 1.0, 0.0)
    gate = _dot(gate_rows, fold, HI) * (1.0 / MOBA_BLOCK)
    sel = jnp.zeros((nq, n_blocks), F32)
    for hit in _topk_blocks(gate, jnp.full((nq, n_blocks), True), 1):
        sel = jnp.where(hit, 1.0, sel)

    head_ok = (lax.broadcasted_iota(jnp.int32, (nq, blk_cols), 0) % N_HEADS
               == lax.broadcasted_iota(jnp.int32, (nq, blk_cols), 1) % N_HEADS)
    orow = lax.broadcasted_iota(jnp.int32, (nq, nq), 0)
    ocol = lax.broadcasted_iota(jnp.int32, (nq, nq), 1)
    own_ok = (orow % N_HEADS == ocol % N_HEADS) & (ocol // N_HEADS <= orow // N_HEADS)
    s_own = jnp.where(own_ok, _dot_nt(q_bf, kn_ref[0].astype(BF16)) * scale, NEG)
    m = jnp.max(s_own, axis=1, keepdims=True)
    for n in range(n_blocks):
        cs = slice(n * blk_cols, (n + 1) * blk_cols)
        s_blk = jnp.where(head_ok & (sel[:, n:n + 1] > 0.5), s_scr[:, cs], NEG)
        s_scr[:, cs] = s_blk
        m = jnp.maximum(m, jnp.max(s_blk, axis=1, keepdims=True))
    p_own = jnp.exp(s_own - m)
    denom = jnp.sum(p_own, axis=1, keepdims=True)
    for n in range(n_blocks):
        cs = slice(n * blk_cols, (n + 1) * blk_cols)
        p_blk = jnp.exp(s_scr[:, cs] - m)
        denom = denom + jnp.sum(p_blk, axis=1, keepdims=True)
        p_scr[:, cs] = p_blk.astype(BF16)

    acc = _dot(p_own.astype(BF16), vn_ref[0].astype(BF16))
    for j in range(n_ck, n_stream):
        slot = j % N_SLOTS
        for cp in copies(b, j, slot):
            cp.wait()
        start_ahead(j)
        jc = j - n_ck
        acc = acc + _dot(p_scr[:, jc * cols:(jc + 1) * cols], buf[slot].reshape(cols, HEAD_DIM).astype(BF16))
        run_stages(j)
    o_ref[0] = acc / denom


def _moba_sample_and_post(pt_flat, q16, kn16, vn16, cache_k4, cache_v4, n_pages,
                          x, o_delta, o_moba, ple, w_out, ffn_norm, w_gate, w_up, w_down, w_ple, w_ple_gate, out_norm):
    nb, nq, _ = q16.shape
    t, d = x.shape
    tm = t // nb
    assert t % nb == 0 and tm % SUBLANE == 0
    page_size = cache_k4.shape[1] * cache_k4.shape[2] // N_HEADS
    n_cols = n_pages * page_size * N_HEADS
    seq = pl.BlockSpec((1, nq, HEAD_DIM), lambda i, pt: (i, 0, 0))
    tok = lambda w: pl.BlockSpec((tm, w), lambda i, pt: (i, 0))
    weights = (w_out, ffn_norm, w_gate, w_up, w_down, w_ple, w_ple_gate, out_norm)
    return pl.pallas_call(
        functools.partial(_moba_sample_kernel, n_pages=n_pages, page_size=page_size),
        grid_spec=pltpu.PrefetchScalarGridSpec(
            num_scalar_prefetch=1,
            grid=(nb,),
            in_specs=[seq, seq, seq, pl.BlockSpec(memory_space=pl.ANY), pl.BlockSpec(memory_space=pl.ANY),
                      tok(d), tok(GROUP_WIDTH), tok(GROUP_WIDTH), tok(ple.shape[1])]
                     + [_resident(w.shape) for w in weights],
            out_specs=[seq, tok(d)],
            scratch_shapes=[pltpu.VMEM((N_SLOTS, PAGES_PER_CHUNK) + cache_k4.shape[1:], F32),
                            pltpu.SemaphoreType.DMA((N_SLOTS,)),
                            pltpu.VMEM((nq, n_cols), F32),
                            pltpu.VMEM((nq, n_cols), BF16),
                            pltpu.VMEM((n_pages, 2 * N_HEADS, HEAD_DIM), F32)] + _post_scratch(tm, d)),
        out_shape=[jax.ShapeDtypeStruct((nb, nq, HEAD_DIM), F32), jax.ShapeDtypeStruct((t, d), F32)],
        compiler_params=_cparams(("arbitrary",)),
        name="moba_sample_post",
    )(pt_flat, q16, kn16, vn16, cache_k4, cache_v4, x, o_delta, o_moba, ple, *weights)


def _rope_tables(pos):
    half = HEAD_DIM // 2
    inv_freq = jnp.exp(jnp.arange(half, dtype=F32) * (-2.0 * math.log(ROPE_THETA) / HEAD_DIM))
    ang = pos.astype(F32)[:, None] * inv_freq[None, :]
    cos, sin = jnp.cos(ang), jnp.sin(ang)
    return jnp.concatenate([cos, cos], axis=-1), jnp.concatenate([-sin, sin], axis=-1)


def _pad_rows(a, n_tok):
    return jnp.pad(a, ((0, 0), (0, SUBLANE - n_tok), (0, 0)))


def _pick_tile(n, pref):
    t = min(pref, n)
    while n % t:
        t //= 2
    return t


def kernel(x_prompt, x_sample, cache_k, cache_v, page_table, state_delta, state_conv, p_prompt, p_sample,
           attn_norm, w_in, conv_w, a_log, dt_bias, delta_norm, w_out, ffn_norm, w_gate, w_up, w_down,
           w_ple, w_ple_gate, final_norm):
    bp, s_len, d = x_prompt.shape
    bs, n_tok, _ = x_sample.shape
    depth = w_in.shape[0]
    n_pages = page_table.shape[1]
    page_size = cache_k.shape[2]
    assert depth == 1, "intermediate layers would need the un-normalised hidden state"
    assert n_tok <= SUBLANE - (CONV_WIDTH - 1) and s_len % MOBA_BLOCK == 0 and s_len >= SUBLANE
    assert (n_pages * page_size) % MOBA_BLOCK == 0 and n_pages % PAGES_PER_CHUNK == 0 and page_size % 2 == 0
    assert (2 * n_pages // PAGES_PER_CHUNK) % N_SLOTS == 0
    assert n_pages * page_size // MOBA_BLOCK >= MOBA_TOPK and s_len // MOBA_BLOCK > MOBA_TOPK

    past = n_pages * page_size
    cos_p, sin_p = _rope_tables(jnp.arange(s_len, dtype=jnp.int32))
    cos_s, sin_s = _rope_tables(past + jnp.arange(n_tok, dtype=jnp.int32))
    cos_s, sin_s = jnp.tile(cos_s, (bs, 1)), jnp.tile(sin_s, (bs, 1))
    pt_flat = page_table.reshape(-1)
    row = lambda v: v.reshape(1, -1)

    hp = x_prompt.reshape(bp * s_len, d)
    hs = x_sample.reshape(bs * n_tok, d)
    tm_p = _pick_tile(s_len, 512)
    tm_s = _pick_tile(bs * n_tok, 128)
    tt = _pick_tile(s_len, SUBLANE * DELTA_CHUNK)
    outs = [[] for _ in range(8)]
    for i in range(depth):
        wi = w_in[i]
        c_dw = CONV_CH + GROUP_WIDTH
        w_in_r = jnp.concatenate(
            [wi[:, :c_dw], wi[:, c_dw + 2 * N_HEADS:], wi[:, c_dw:c_dw + 2 * N_HEADS],
             jnp.zeros((d, LANE - 2 * N_HEADS), F32)], axis=1).astype(BF16)
        alog_row = jnp.zeros((1, LANE), F32).at[0, N_HEADS:2 * N_HEADS].set(a_log[i])
        dtb_row = jnp.zeros((1, LANE), F32).at[0, N_HEADS:2 * N_HEADS].set(dt_bias[i])
        wb = lambda w: w[i].astype(BF16)
        post_w = (wb(w_out), row(ffn_norm[i]), wb(w_gate), wb(w_up), wb(w_down), wb(w_ple), wb(w_ple_gate))
        out_norm = row(final_norm)

        qkv, z, q_m, k_m, v_m, gates, k_bf, vt_bf, ksum = _in_proj(hp, row(attn_norm[i]), w_in_r, cos_p, sin_p, tm_p)
        sh = lambda a: a.reshape(bp, s_len, a.shape[-1])
        qkv3 = sh(qkv)
        hist0 = jnp.zeros((bp, CONV_WIDTH - 1, CONV_CH), F32)
        u, wk, qd, kd, qk, cd = _delta_prep(qkv3, sh(gates), hist0, conv_w[i], alog_row, dtb_row, tt)
        s00 = jnp.zeros((bp, N_HEADS, HEAD_DIM, HEAD_DIM), F32)
        o_delta, d_p = _delta_scan(u, wk, qd, kd, qk, cd, sh(z), s00, row(delta_norm[i]), tt)
        nblk = s_len // MOBA_BLOCK
        kmean = ksum.reshape(bp, s_len // tm_p, -1, GROUP_WIDTH)
        kmean = kmean.reshape(bp, -1, GROUP_WIDTH)
        if tm_p < MOBA_BLOCK:
            kmean = kmean.reshape(bp, nblk, MOBA_BLOCK // tm_p, GROUP_WIDTH).sum(axis=2)
        kmean = kmean * (1.0 / MOBA_BLOCK)
        o_moba = _moba_prompt(sh(q_m), kmean, sh(k_bf), vt_bf)
        prompt_post_in = (hp, o_delta.reshape(bp * s_len, GROUP_WIDTH), o_moba.reshape(bp * s_len, GROUP_WIDTH),
                          p_prompt[i].reshape(bp * s_len, -1))
        outs[0].append(k_m.reshape(bp, s_len, N_HEADS, HEAD_DIM))
        outs[1].append(v_m.reshape(bp, s_len, N_HEADS, HEAD_DIM))
        outs[4].append(d_p)
        outs[6].append(qkv3[:, s_len - (CONV_WIDTH - 1):])

        qkv, z, q_m, k_m, v_m, gates, _, _, _ = _in_proj(hs, row(attn_norm[i]), w_in_r, cos_s, sin_s, tm_s)
        sh = lambda a: a.reshape(bs, n_tok, a.shape[-1])
        qkv3 = sh(qkv)
        xx_full = jnp.concatenate([state_conv[i], qkv3], axis=1)
        xx = jnp.pad(xx_full, ((0, 0), (0, SUBLANE - xx_full.shape[1]), (0, 0)))
        nseq = _pick_tile(bs, 8)
        o_delta8, d_s = _delta_sample(xx, _pad_rows(sh(gates), n_tok), _pad_rows(sh(z), n_tok), state_delta[i],
                                      conv_w[i], alog_row, dtb_row, row(delta_norm[i]), n_tok, nseq)
        n_pool = cache_k.shape[1]
        page_view = (n_pool, page_size // 2, 2 * N_HEADS, HEAD_DIM)
        rows = lambda a: a.reshape(bs, n_tok * N_HEADS, HEAD_DIM)
        o_moba_s, hp = _moba_sample_and_post(pt_flat, rows(q_m), rows(k_m), rows(v_m),
                                             cache_k[i].reshape(page_view), cache_v[i].reshape(page_view), n_pages,
                                             *prompt_post_in, *post_w, out_norm)
        hs_new = _post(hs, o_delta8[:, :n_tok].reshape(bs * n_tok, GROUP_WIDTH),
                       o_moba_s.reshape(bs * n_tok, GROUP_WIDTH),
                       p_sample[i].reshape(bs * n_tok, -1), *post_w, out_norm, tm_s)
        outs[2].append(k_m.reshape(bs, n_tok, N_HEADS, HEAD_DIM))
        outs[3].append(v_m.reshape(bs, n_tok, N_HEADS, HEAD_DIM))
        outs[5].append(d_s)
        outs[7].append(xx_full[:, xx_full.shape[1] - (CONV_WIDTH - 1):])
        hs = hs_new

    y_prompt = hp.reshape(bp, s_len, d)
    y_sample = hs.reshape(bs, n_tok, d)
    st = [jnp.stack(o) for o in outs]
    return (y_prompt, y_sample, st[0], st[1], st[2], st[3], st[4], st[5], st[6], st[7])
```

```python
import functools
import math

import jax
import jax.numpy as jnp
from jax import lax
from jax.experimental import pallas as pl
from jax.experimental.pallas import tpu as pltpu

F32 = jnp.float32
BF16 = jnp.bfloat16
HI = lax.Precision.HIGHEST

HEAD_DIM = 128
N_HEADS = 4
GROUP_WIDTH = N_HEADS * HEAD_DIM
CONV_WIDTH = 4
CONV_CH = 3 * GROUP_WIDTH
DELTA_CHUNK = 64
MOBA_BLOCK = 256
MOBA_TOPK = 3
ROPE_THETA = 10000.0
RMS_EPS = 1e-6
L2_EPS = 1e-6
NEG = -1e30
LANE = 128
SUBLANE = 8
VMEM_LIMIT = 56 * 1024 * 1024

C_QKV = 0
C_Z = CONV_CH
C_QM = C_Z + GROUP_WIDTH
C_KM = C_QM + GROUP_WIDTH
C_VM = C_KM + GROUP_WIDTH
C_GATE = C_VM + GROUP_WIDTH
IN_COLS = C_GATE + LANE


def _cparams(sem):
    return pltpu.CompilerParams(dimension_semantics=sem, vmem_limit_bytes=VMEM_LIMIT)


def _resident(shape):
    nd = len(shape)
    return pl.BlockSpec(shape, lambda *_: (0,) * nd, pipeline_mode=pl.Buffered(1))


def _sigmoid(x):
    return 1.0 / (1.0 + jnp.exp(-x))


def _silu(x):
    return x * _sigmoid(x)


def _softplus(x):
    return jnp.maximum(x, 0.0) + jnp.log(1.0 + jnp.exp(-jnp.abs(x)))


def _rms(x, g):
    return x * lax.rsqrt(jnp.mean(x * x, axis=-1, keepdims=True) + RMS_EPS) * g


def _dot(a, b, prec=None):
    return jnp.dot(a, b, precision=prec, preferred_element_type=F32)


def _dot_nt(a, b, prec=None):
    return lax.dot_general(a, b, (((1,), (1,)), ((), ())), precision=prec, preferred_element_type=F32)


def _bmm(a, b, prec=None):
    return lax.dot_general(a, b, (((2,), (1,)), ((0,), (0,))), precision=prec, preferred_element_type=F32)


def _bmm_nt(a, b, prec=None):
    return lax.dot_general(a, b, (((2,), (2,)), ((0,), (0,))), precision=prec, preferred_element_type=F32)


def _bmm_tn(a, b, prec=None):
    return lax.dot_general(a, b, (((1,), (1,)), ((0,), (0,))), precision=prec, preferred_element_type=F32)


def _in_proj_kernel(x_ref, g_ref, w_ref, cos_ref, sin_ref,
                    qkv_ref, z_ref, q_ref, k_ref, v_ref, gate_ref, kbf_ref, vt_ref, ksum_ref):
    a = _rms(x_ref[...], g_ref[...]).astype(BF16)
    cos = cos_ref[...]
    sin = sin_ref[...]

    def proj(c0, width):
        return _dot(a, w_ref[:, c0:c0 + width])

    def rope(t):
        heads = []
        for h in range(N_HEADS):
            th = t[:, h * HEAD_DIM:(h + 1) * HEAD_DIM]
            heads.append(th * cos + pltpu.roll(th, HEAD_DIM // 2, axis=1) * sin)
        return jnp.concatenate(heads, axis=1)

    qkv_ref[...] = proj(C_QKV, CONV_CH)
    z_ref[...] = proj(C_Z, GROUP_WIDTH)
    q_ref[...] = rope(proj(C_QM, GROUP_WIDTH))
    k = rope(proj(C_KM, GROUP_WIDTH))
    v = proj(C_VM, GROUP_WIDTH)
    tm = k.shape[0]
    for h in range(N_HEADS):
        k_ref[pl.ds(h, tm, stride=N_HEADS), :] = k[:, h * HEAD_DIM:(h + 1) * HEAD_DIM]
        v_ref[pl.ds(h, tm, stride=N_HEADS), :] = v[:, h * HEAD_DIM:(h + 1) * HEAD_DIM]
    kbf_ref[...] = k.astype(BF16)
    vt_ref[0] = v.T.astype(BF16)
    nblk = ksum_ref.shape[1]
    ksum_ref[0] = jnp.sum(k.reshape(nblk, k.shape[0] // nblk, GROUP_WIDTH), axis=1)
    gate_ref[...] = proj(C_GATE, LANE)


def _in_proj(x, norm_w, w_in_r, cos, sin, tm):
    t, d = x.shape
    nt = t // tm
    tiles_per_seq = cos.shape[0] // tm
    n_tab = tiles_per_seq
    nblk = max(tm // MOBA_BLOCK, 1)
    tok = lambda w: pl.BlockSpec((tm, w), lambda i: (i, 0))
    tab = pl.BlockSpec((tm, HEAD_DIM), lambda i: (i % n_tab, 0))
    out_shapes = [
        jax.ShapeDtypeStruct((t, CONV_CH), F32), jax.ShapeDtypeStruct((t, GROUP_WIDTH), F32),
        jax.ShapeDtypeStruct((t, GROUP_WIDTH), F32), jax.ShapeDtypeStruct((t * N_HEADS, HEAD_DIM), F32),
        jax.ShapeDtypeStruct((t * N_HEADS, HEAD_DIM), F32), jax.ShapeDtypeStruct((t, LANE), F32),
        jax.ShapeDtypeStruct((t, GROUP_WIDTH), BF16),
        jax.ShapeDtypeStruct((nt // tiles_per_seq, GROUP_WIDTH, tiles_per_seq * tm), BF16),
        jax.ShapeDtypeStruct((nt, nblk, GROUP_WIDTH), F32),
    ]
    head_rows = pl.BlockSpec((tm * N_HEADS, HEAD_DIM), lambda i: (i, 0))
    out_specs = [tok(CONV_CH), tok(GROUP_WIDTH), tok(GROUP_WIDTH), head_rows, head_rows,
                 tok(LANE), tok(GROUP_WIDTH),
                 pl.BlockSpec((1, GROUP_WIDTH, tm), lambda i: (i // tiles_per_seq, 0, i % tiles_per_seq)),
                 pl.BlockSpec((1, nblk, GROUP_WIDTH), lambda i: (i, 0, 0))]
    return pl.pallas_call(
        _in_proj_kernel,
        grid=(nt,),
        in_specs=[tok(d), _resident((1, d)), _resident(w_in_r.shape), tab, tab],
        out_specs=out_specs,
        out_shape=out_shapes,
        compiler_params=_cparams(("arbitrary",)),
        name="in_proj",
    )(x, norm_w, w_in_r, cos, sin)


def _split(x):
    hi = x.astype(BF16)
    return hi, (x - hi.astype(F32)).astype(BF16)


def _bmm_split(a, b, nt=False):
    ah, al = _split(a)
    bh, bl = _split(b)
    lhs = jnp.concatenate([ah, al, ah], axis=2)
    if nt:
        return _bmm_nt(lhs, jnp.concatenate([bh, bh, bl], axis=2))
    return _bmm(lhs, jnp.concatenate([bh, bh, bl], axis=1))


def _bmm_exact(a, b, nt=False):
    return _bmm_nt(a, b, HI) if nt else _bmm(a, b, HI)


def _chunk_prep(q, k, v, g, beta, mm):
    n, c, dh = q.shape
    row = lax.broadcasted_iota(jnp.int32, (c, c), 0)
    col = lax.broadcasted_iota(jnp.int32, (c, c), 1)
    causal = row >= col
    strict = row > col
    gx = jnp.concatenate([jnp.broadcast_to(g, (n, c, dh)),
                          jnp.where(strict, jnp.broadcast_to(g, (n, c, c)), 0.0)], axis=2)
    g1 = gx.astype(BF16)
    r1 = gx - g1.astype(F32)
    g2 = r1.astype(BF16)
    g3 = (r1 - g2.astype(F32)).astype(BF16)
    lower3 = jnp.broadcast_to(jnp.concatenate([causal.astype(BF16)] * 3, axis=1), (n, c, 3 * c))
    csum = _bmm(lower3, jnp.concatenate([g1, g2, g3], axis=1))
    gcb = csum[:, :, :dh]
    dlog = csum[:, :, dh:]
    decay = jnp.where(causal, jnp.exp(jnp.where(causal, dlog, 0.0)), 0.0)
    kb = k * beta
    kq = mm(jnp.concatenate([kb, q], axis=1), k, nt=True)
    a_mat = jnp.where(strict, kq[:, :c] * decay, 0.0)
    qk = jnp.where(causal, kq[:, c:] * decay, 0.0)
    nmat = -a_mat
    t_inv = jnp.where(row == col, 1.0, 0.0) + nmat
    n_factors = int(math.log2(c))
    if n_factors > 1:
        nmat = mm(nmat, nmat)
    for i in range(1, n_factors):
        if i + 1 < n_factors:
            both = mm(jnp.concatenate([nmat, t_inv], axis=1), nmat)
            nmat, t_inv = both[:, :c], t_inv + both[:, c:]
        else:
            t_inv = t_inv + mm(t_inv, nmat)
    egc = jnp.exp(gcb)
    uw = mm(t_inv, jnp.concatenate([v * beta, kb * egc], axis=2))
    g_last = gcb[:, c - 1:c, :]
    return uw[:, :, :dh], uw[:, :, dh:], qk, q * egc, k * jnp.exp(g_last - gcb), jnp.exp(g_last)


def _chunk_apply(s, u, wk, qk, qd, kd, cd):
    c = u.shape[1]
    xs = _bmm(jnp.concatenate([wk, qd], axis=1).astype(BF16), s.astype(BF16))
    w = u - xs[:, :c]
    wb = w.astype(BF16)
    o = xs[:, c:] + _bmm(qk.astype(BF16), wb)
    s_new = s * cd + _bmm_tn(kd.astype(BF16), wb)
    return o, s_new


def _delta_inputs(y, gates, alog, dtb, h):
    sl = lambda base: slice(base + h * HEAD_DIM, base + (h + 1) * HEAD_DIM)
    q = y[:, sl(0)]
    k = y[:, sl(GROUP_WIDTH)]
    v = y[:, sl(2 * GROUP_WIDTH)]
    q = q * lax.rsqrt(jnp.sum(q * q, axis=-1, keepdims=True) + L2_EPS) * (HEAD_DIM ** -0.5)
    k = k * lax.rsqrt(jnp.sum(k * k, axis=-1, keepdims=True) + L2_EPS)
    beta = _sigmoid(gates[:, h:h + 1])
    g = -jnp.exp(alog[:, N_HEADS + h:N_HEADS + h + 1]) * _softplus(
        gates[:, N_HEADS + h:N_HEADS + h + 1] + dtb[:, N_HEADS + h:N_HEADS + h + 1])
    return q, k, v, g, beta


def _delta_prep_kernel(qkv_ref, gate_ref, hist_ref, convw_ref, alog_ref, dtb_ref,
                       u_ref, wk_ref, qd_ref, kd_ref, qk_ref, cd_ref, prev_ref):
    tt = qkv_ref.shape[1]
    n = tt // DELTA_CHUNK

    @pl.when(pl.program_id(1) == 0)
    def _():
        prev_ref[...] = jnp.zeros_like(prev_ref)
        prev_ref[SUBLANE - (CONV_WIDTH - 1):, :] = hist_ref[0]

    x = qkv_ref[0]
    prev = prev_ref[...]
    w = convw_ref[...]
    first_rows = lax.broadcasted_iota(jnp.int32, (SUBLANE, 1), 0)
    y = x * w[CONV_WIDTH - 1:CONV_WIDTH]
    for back in range(1, CONV_WIDTH):
        shifted = pltpu.roll(x, back, axis=0)
        head = jnp.where(first_rows < back, pltpu.roll(prev, back, axis=0), shifted[:SUBLANE])
        shifted = jnp.concatenate([head, shifted[SUBLANE:]], axis=0)
        y = y + shifted * w[CONV_WIDTH - 1 - back:CONV_WIDTH - back]
    y = _silu(y)
    prev_ref[...] = x[tt - SUBLANE:]

    gates = gate_ref[0]
    alog = alog_ref[...]
    dtb = dtb_ref[...]
    for h in range(N_HEADS):
        q, k, v, g, beta = _delta_inputs(y, gates, alog, dtb, h)
        r3 = lambda t: t.reshape(n, DELTA_CHUNK, t.shape[-1])
        u, wk, qk, qd, kd, cd = _chunk_prep(r3(q), r3(k), r3(v), r3(g), r3(beta), _bmm_split)
        hs = slice(h * HEAD_DIM, (h + 1) * HEAD_DIM)
        u_ref[0, :, hs] = u.reshape(tt, HEAD_DIM)
        wk_ref[0, :, hs] = wk.reshape(tt, HEAD_DIM)
        qd_ref[0, :, hs] = qd.reshape(tt, HEAD_DIM)
        kd_ref[0, :, hs] = kd.reshape(tt, HEAD_DIM)
        qk_ref[0, :, h * DELTA_CHUNK:(h + 1) * DELTA_CHUNK] = qk.reshape(tt, DELTA_CHUNK)
        cd_ref[0, :, hs] = cd.reshape(n, HEAD_DIM)


def _delta_prep(qkv, gates, hist, conv_w, alog_row, dtb_row, tt):
    b, s, _ = qkv.shape
    n = tt // DELTA_CHUNK
    tok = lambda w: pl.BlockSpec((1, tt, w), lambda bi, j: (bi, j, 0))
    wide = jax.ShapeDtypeStruct((b, s, GROUP_WIDTH), F32)
    return pl.pallas_call(
        _delta_prep_kernel,
        grid=(b, s // tt),
        in_specs=[tok(CONV_CH), tok(LANE),
                  pl.BlockSpec((1, CONV_WIDTH - 1, CONV_CH), lambda bi, j: (bi, 0, 0)),
                  _resident(conv_w.shape), _resident((1, LANE)), _resident((1, LANE))],
        out_specs=[tok(GROUP_WIDTH), tok(GROUP_WIDTH), tok(GROUP_WIDTH), tok(GROUP_WIDTH),
                   tok(N_HEADS * DELTA_CHUNK),
                   pl.BlockSpec((1, n, GROUP_WIDTH), lambda bi, j: (bi, j, 0))],
        out_shape=[wide, wide, wide, wide,
                   jax.ShapeDtypeStruct((b, s, N_HEADS * DELTA_CHUNK), F32),
                   jax.ShapeDtypeStruct((b, s // DELTA_CHUNK, GROUP_WIDTH), F32)],
        scratch_shapes=[pltpu.VMEM((SUBLANE, CONV_CH), F32)],
        compiler_params=_cparams(("arbitrary", "arbitrary")),
        name="delta_prep",
    )(qkv, gates, hist, conv_w, alog_row, dtb_row)


def _delta_scan_kernel(u_ref, wk_ref, qd_ref, kd_ref, qk_ref, cd_ref, z_ref, s0_ref, dnorm_ref,
                       o_ref, sfin_ref, s_scr):
    nb, tt, _ = u_ref.shape
    n = tt // DELTA_CHUNK

    @pl.when(pl.program_id(0) == 0)
    def _():
        s_scr[...] = s0_ref[...]

    dnorm = dnorm_ref[...]
    for ci in range(n):
        rows = slice(ci * DELTA_CHUNK, (ci + 1) * DELTA_CHUNK)
        for b in range(nb):
            for h in range(N_HEADS):
                hs = slice(h * HEAD_DIM, (h + 1) * HEAD_DIM)
                o, s_new = _chunk_apply(
                    s_scr[b, h][None], u_ref[b, rows, hs][None], wk_ref[b, rows, hs][None],
                    qk_ref[b, rows, h * DELTA_CHUNK:(h + 1) * DELTA_CHUNK][None],
                    qd_ref[b, rows, hs][None], kd_ref[b, rows, hs][None], cd_ref[b, ci:ci + 1, hs][None])
                s_scr[b, h] = s_new[0]
                o_ref[b, rows, hs] = _rms(o[0], dnorm) * _silu(z_ref[b, rows, hs])

    @pl.when(pl.program_id(0) == pl.num_programs(0) - 1)
    def _():
        sfin_ref[...] = s_scr[...]


def _delta_scan(u, wk, qd, kd, qk, cd, z, s0, dnorm, tt):
    b, s, _ = u.shape
    n = tt // DELTA_CHUNK
    tok = lambda w: pl.BlockSpec((b, tt, w), lambda j: (0, j, 0))
    st = pl.BlockSpec(s0.shape, lambda j: (0, 0, 0, 0))
    return pl.pallas_call(
        _delta_scan_kernel,
        grid=(s // tt,),
        in_specs=[tok(GROUP_WIDTH), tok(GROUP_WIDTH), tok(GROUP_WIDTH), tok(GROUP_WIDTH),
                  tok(N_HEADS * DELTA_CHUNK), pl.BlockSpec((b, n, GROUP_WIDTH), lambda j: (0, j, 0)),
                  tok(GROUP_WIDTH), st, _resident((1, HEAD_DIM))],
        out_specs=[tok(GROUP_WIDTH), st],
        out_shape=[jax.ShapeDtypeStruct((b, s, GROUP_WIDTH), F32), jax.ShapeDtypeStruct(s0.shape, F32)],
        scratch_shapes=[pltpu.VMEM(s0.shape, F32)],
        compiler_params=_cparams(("arbitrary",)),
        name="delta_scan",
    )(u, wk, qd, kd, qk, cd, z, s0, dnorm)


def _topk_blocks(gate, valid, axis):
    nb = gate.shape[axis]
    blk = lax.broadcasted_iota(jnp.int32, gate.shape, axis).astype(F32)
    gv = jnp.where(valid, gate, -jnp.inf)
    hits = []
    for _ in range(MOBA_TOPK):
        top = jnp.max(gv, axis=axis, keepdims=True)
        first = jnp.min(jnp.where(gv == top, blk, float(nb)), axis=axis, keepdims=True)
        hit = (blk == first) & (top > -jnp.inf)
        gv = jnp.where(hit, -jnp.inf, gv)
        hits.append(hit)
    return hits


def _moba_prompt_kernel(q_ref, kmean_ref, k_ref, vt_ref, o_ref,
                        bias_scr, qbf_scr, m_scr, l_scr, acc_scr, sa_scr, sb_scr):
    cur = pl.program_id(1)
    tq = q_ref.shape[1]
    nb = kmean_ref.shape[1]
    scale = HEAD_DIM ** -0.5
    key_i = lax.broadcasted_iota(jnp.int32, (MOBA_BLOCK, tq), 0)
    qry_i = lax.broadcasted_iota(jnp.int32, (MOBA_BLOCK, tq), 1)
    causal_bias = jnp.where(key_i <= qry_i, 0.0, NEG)
    blk_ids = lax.broadcasted_iota(jnp.int32, (nb, tq), 0)
    heads = [slice(h * HEAD_DIM, (h + 1) * HEAD_DIM) for h in range(N_HEADS)]

    for h, hs in enumerate(heads):
        q = q_ref[0, :, hs]
        gate_t = _dot_nt(kmean_ref[0, :, hs], q, HI)
        chosen = jnp.zeros((nb, tq), jnp.bool_)
        for hit in _topk_blocks(gate_t, blk_ids < cur, 0):
            chosen = chosen | hit
        bias_scr[h] = jnp.where(chosen, 0.0, NEG)
        qbf_scr[h] = (q * scale).astype(BF16)
        m_scr[h] = jnp.full((1, tq), NEG, F32)
        l_scr[h] = jnp.zeros((1, tq), F32)
        acc_scr[h] = jnp.zeros((HEAD_DIM, tq), F32)

    def scores(jb, s_buf):
        r0 = pl.multiple_of(jb * MOBA_BLOCK, MOBA_BLOCK)
        for h, hs in enumerate(heads):
            s_buf[h] = _dot_nt(k_ref[0, pl.ds(r0, MOBA_BLOCK), hs], qbf_scr[h])

    def update(jb, s_buf, bias_of):
        r0 = pl.multiple_of(jb * MOBA_BLOCK, MOBA_BLOCK)
        for h, hs in enumerate(heads):
            s = s_buf[h] + bias_of(h)
            m_i = m_scr[h]
            m_new = jnp.maximum(m_i, jnp.max(s, axis=0, keepdims=True))
            alpha = jnp.exp(m_i - m_new)
            p = jnp.exp(s - m_new)
            m_scr[h] = m_new
            l_scr[h] = alpha * l_scr[h] + jnp.sum(p, axis=0, keepdims=True)
            acc_scr[h] = alpha * acc_scr[h] + _dot(vt_ref[0, hs, pl.ds(r0, MOBA_BLOCK)], p.astype(BF16))

    def chosen_bias(jb):
        return lambda h: bias_scr[h, pl.ds(jb, 1), :]

    def own_bias(h):
        return causal_bias

    scores(0, sa_scr)

    def block_pair(pair, carry):
        jb = 2 * pair
        scores(jb + 1, sb_scr)
        update(jb, sa_scr, chosen_bias(jb))
        scores(jb + 2, sa_scr)
        update(jb + 1, sb_scr, chosen_bias(jb + 1))
        return carry

    lax.fori_loop(0, cur // 2, block_pair, 0)

    @pl.when(cur % 2 == 1)
    def _():
        scores(cur, sb_scr)
        update(cur - 1, sa_scr, chosen_bias(cur - 1))
        update(cur, sb_scr, own_bias)

    @pl.when(cur % 2 == 0)
    def _():
        update(cur, sa_scr, own_bias)

    for h, hs in enumerate(heads):
        o_ref[0, :, hs] = (acc_scr[h] / l_scr[h]).T


def _moba_prompt(q, kmean, k_bf, vt_bf):
    b, s, _ = q.shape
    nb = s // MOBA_BLOCK
    tok = pl.BlockSpec((1, MOBA_BLOCK, GROUP_WIDTH), lambda bi, j: (bi, j, 0))
    seq = lambda a: pl.BlockSpec((1,) + a.shape[1:], lambda bi, j: (bi, 0, 0))
    return pl.pallas_call(
        _moba_prompt_kernel,
        grid=(b, nb),
        in_specs=[tok, seq(kmean), seq(k_bf), seq(vt_bf)],
        out_specs=tok,
        out_shape=jax.ShapeDtypeStruct((b, s, GROUP_WIDTH), F32),
        scratch_shapes=[pltpu.VMEM((N_HEADS, nb, MOBA_BLOCK), F32),
                        pltpu.VMEM((N_HEADS, MOBA_BLOCK, HEAD_DIM), BF16),
                        pltpu.VMEM((N_HEADS, 1, MOBA_BLOCK), F32),
                        pltpu.VMEM((N_HEADS, 1, MOBA_BLOCK), F32),
                        pltpu.VMEM((N_HEADS, HEAD_DIM, MOBA_BLOCK), F32),
                        pltpu.VMEM((N_HEADS, MOBA_BLOCK, MOBA_BLOCK), F32),
                        pltpu.VMEM((N_HEADS, MOBA_BLOCK, MOBA_BLOCK), F32)],
        compiler_params=_cparams(("arbitrary", "arbitrary")),
        name="moba_prompt",
    )(q, kmean, k_bf, vt_bf)


FF_CHUNK = 256
N_POST_STAGES = 8


def _post_stages(x_ref, od_ref, om_ref, ple_ref, wout_ref, fnorm_ref, wg_ref, wu_ref, wd_ref,
                 wple_ref, wpg_ref, onorm_ref, y_ref, h_scr, f_scr, ffn_scr):
    d_ff = wg_ref.shape[1]
    ff = FF_CHUNK if d_ff % FF_CHUNK == 0 else d_ff
    starts = list(range(0, d_ff, ff))
    n_mid = N_POST_STAGES - 2
    groups = [starts[i * len(starts) // n_mid:(i + 1) * len(starts) // n_mid] for i in range(n_mid)]

    def head():
        mix = jnp.concatenate([od_ref[...], om_ref[...]], axis=1).astype(BF16)
        h = x_ref[...] + _dot(mix, wout_ref[...])
        h_scr[...] = h
        f_scr[...] = _rms(h, fnorm_ref[...]).astype(BF16)
        ffn_scr[...] = jnp.zeros_like(ffn_scr)

    def ffn(group):
        def run():
            f = f_scr[...]
            part = None
            for c0 in group:
                gate = _dot(f, wg_ref[:, c0:c0 + ff])
                up = _dot(f, wu_ref[:, c0:c0 + ff])
                down = _dot((_silu(gate) * up).astype(BF16), wd_ref[c0:c0 + ff, :])
                part = down if part is None else part + down
            if part is not None:
                ffn_scr[...] += part
        return run

    def tail():
        h = h_scr[...] + ffn_scr[...]
        h = h + _dot(ple_ref[...].astype(BF16), wple_ref[...]) * _sigmoid(_dot(h.astype(BF16), wpg_ref[...]))
        y_ref[...] = _rms(h, onorm_ref[...])

    return [head] + [ffn(g) for g in groups] + [tail]


def _post_scratch(tm, d):
    return [pltpu.VMEM((tm, d), F32), pltpu.VMEM((tm, d), BF16), pltpu.VMEM((tm, d), F32)]


def _post_kernel(*refs):
    for stage in _post_stages(*refs):
        stage()


def _post(x, o_delta, o_moba, ple, w_out, ffn_norm, w_gate, w_up, w_down, w_ple, w_ple_gate, out_norm, tm):
    t, d = x.shape
    tok = lambda w: pl.BlockSpec((tm, w), lambda i: (i, 0))
    return pl.pallas_call(
        _post_kernel,
        grid=(t // tm,),
        in_specs=[tok(d), tok(GROUP_WIDTH), tok(GROUP_WIDTH), tok(ple.shape[1]),
                  _resident(w_out.shape), _resident((1, d)), _resident(w_gate.shape), _resident(w_up.shape),
                  _resident(w_down.shape), _resident(w_ple.shape), _resident(w_ple_gate.shape), _resident((1, d))],
        out_specs=tok(d),
        out_shape=jax.ShapeDtypeStruct((t, d), F32),
        scratch_shapes=_post_scratch(tm, d),
        compiler_params=_cparams(("arbitrary",)),
        name="post",
    )(x, o_delta, o_moba, ple, w_out, ffn_norm, w_gate, w_up, w_down, w_ple, w_ple_gate, out_norm)


def _delta_sample_kernel(xx_ref, gate_ref, z_ref, s_ref, convw_ref, alog_ref, dtb_ref, dnorm_ref,
                         o_ref, snew_ref, *, n_tok):
    nseq = xx_ref.shape[0]
    rows = nseq * SUBLANE
    xx = xx_ref[...].reshape(rows, CONV_CH)
    w = convw_ref[...]
    y = xx * w[0:1]
    for j in range(1, CONV_WIDTH):
        y = y + pltpu.roll(xx, rows - j, axis=0) * w[j:j + 1]
    y = _silu(y)
    gates = gate_ref[...].reshape(rows, LANE)
    real = (lax.broadcasted_iota(jnp.int32, (rows, 1), 0) % SUBLANE) < n_tok
    dnorm = dnorm_ref[...]
    for h in range(N_HEADS):
        q, k, v, g, beta = _delta_inputs(y, gates, alog_ref[...], dtb_ref[...], h)
        g = jnp.where(real, g, 0.0)
        beta = jnp.where(real, beta, 0.0)
        r3 = lambda t: t.reshape(nseq, SUBLANE, t.shape[-1])
        o, s_new = _chunk_apply(s_ref[:, h], *_chunk_prep(r3(q), r3(k), r3(v), r3(g), r3(beta), _bmm_exact))
        snew_ref[:, h] = s_new
        hs = slice(h * HEAD_DIM, (h + 1) * HEAD_DIM)
        o_ref[:, :, hs] = _rms(o, dnorm) * _silu(z_ref[:, :, hs])


def _delta_sample(xx, gates8, z8, state, conv_w, alog_row, dtb_row, dnorm, n_tok, nseq):
    nb = xx.shape[0]
    seq = lambda w: pl.BlockSpec((nseq, SUBLANE, w), lambda i: (i, 0, 0))
    st = pl.BlockSpec((nseq,) + state.shape[1:], lambda i: (i, 0, 0, 0))
    return pl.pallas_call(
        functools.partial(_delta_sample_kernel, n_tok=n_tok),
        grid=(nb // nseq,),
        in_specs=[seq(CONV_CH), seq(LANE), seq(GROUP_WIDTH), st, _resident(conv_w.shape),
                  _resident((1, LANE)), _resident((1, LANE)), _resident((1, HEAD_DIM))],
        out_specs=[seq(GROUP_WIDTH), st],
        out_shape=[jax.ShapeDtypeStruct((nb, SUBLANE, GROUP_WIDTH), F32), jax.ShapeDtypeStruct(state.shape, F32)],
        compiler_params=_cparams(("arbitrary",)),
        name="delta_sample",
    )(xx, gates8, z8, state, conv_w, alog_row, dtb_row, dnorm)


PAGES_PER_CHUNK = 16
N_SLOTS = 4


def _moba_sample_kernel(pt_ref, q_ref, kn_ref, vn_ref, kc_ref, vc_ref, *rest, n_pages, page_size):
    post_in, (o_ref, y_ref), (buf, sem, s_scr, p_scr, psum, h_scr, f_scr, ffn_scr) = rest[:12], rest[12:14], rest[14:]
    stages = _post_stages(*post_in, y_ref, h_scr, f_scr, ffn_scr)
    b = pl.program_id(0)
    nseq = pl.num_programs(0)
    n_ck = n_pages // PAGES_PER_CHUNK
    n_stream = 2 * n_ck
    rows_per_page = page_size * N_HEADS
    cols = PAGES_PER_CHUNK * rows_per_page
    blk_cols = MOBA_BLOCK * N_HEADS
    n_blocks = n_pages * page_size // MOBA_BLOCK
    nq = q_ref.shape[1]
    scale = HEAD_DIM ** -0.5

    def copies(seq, j, slot):
        src = kc_ref if j < n_ck else vc_ref
        first = seq * n_pages + (j % n_ck) * PAGES_PER_CHUNK
        return [pltpu.make_async_copy(src.at[pt_ref[first + i]], buf.at[slot, i], sem.at[slot])
                for i in range(PAGES_PER_CHUNK)]

    def start_ahead(j):
        ahead = j + N_SLOTS - 1
        if ahead < n_stream:
            for cp in copies(b, ahead, ahead % N_SLOTS):
                cp.start()
        else:
            @pl.when(b + 1 < nseq)
            def _():
                for cp in copies(b + 1, ahead - n_stream, ahead % N_SLOTS):
                    cp.start()

    @pl.when(b == 0)
    def _():
        for j in range(N_SLOTS - 1):
            for cp in copies(0, j, j):
                cp.start()

    def run_stages(j):
        for stage in stages[j * len(stages) // n_stream:(j + 1) * len(stages) // n_stream]:
            stage()

    q = q_ref[0]
    q_bf = q.astype(BF16)

    for j in range(n_ck):
        slot = j % N_SLOTS
        for cp in copies(b, j, slot):
            cp.wait()
        start_ahead(j)
        pages = buf[slot]
        psum[j * PAGES_PER_CHUNK:(j + 1) * PAGES_PER_CHUNK] = jnp.sum(pages, axis=1)
        s_scr[:, j * cols:(j + 1) * cols] = _dot_nt(q_bf, pages.reshape(cols, HEAD_DIM).astype(BF16)) * scale
        run_stages(j)

    prow = n_pages * 2 * N_HEADS
    gate_rows = _dot_nt(q, psum[...].reshape(prow, HEAD_DIM), HI)
    r_head = lax.broadcasted_iota(jnp.int32, (nq, prow), 0) % N_HEADS
    c_head = lax.broadcasted_iota(jnp.int32, (nq, prow), 1) % N_HEADS
    gate_rows = jnp.where(r_head == c_head, gate_rows, 0.0)
    rows_per_block = prow // n_blocks
    fold = jnp.where(lax.broadcasted_iota(jnp.int32, (prow, n_blocks), 0) // rows_per_block
                     == lax.broadcasted_iota(jnp.int32, (prow, n_blocks), 1), 1.0, 0.0)
    gate = _dot(gate_rows, fold, HI) * (1.0 / MOBA_BLOCK)
    sel = jnp.zeros((nq, n_blocks), F32)
    for hit in _topk_blocks(gate, jnp.full((nq, n_blocks), True), 1):
        sel = jnp.where(hit, 1.0, sel)

    head_ok = (lax.broadcasted_iota(jnp.int32, (nq, blk_cols), 0) % N_HEADS
               == lax.broadcasted_iota(jnp.int32, (nq, blk_cols), 1) % N_HEADS)
    orow = lax.broadcasted_iota(jnp.int32, (nq, nq), 0)
    ocol = lax.broadcasted_iota(jnp.int32, (nq, nq), 1)
    own_ok = (orow % N_HEADS == ocol % N_HEADS) & (ocol // N_HEADS <= orow // N_HEADS)
    s_own = jnp.where(own_ok, _dot_nt(q_bf, kn_ref[0].astype(BF16)) * scale, NEG)
    m = jnp.max(s_own, axis=1, keepdims=True)
    for n in range(n_blocks):
        cs = slice(n * blk_cols, (n + 1) * blk_cols)
        s_blk = jnp.where(head_ok & (sel[:, n:n + 1] > 0.5), s_scr[:, cs], NEG)
        s_scr[:, cs] = s_blk
        m = jnp.maximum(m, jnp.max(s_blk, axis=1, keepdims=True))
    p_own = jnp.exp(s_own - m)
    denom = jnp.sum(p_own, axis=1, keepdims=True)
    for n in range(n_blocks):
        cs = slice(n * blk_cols, (n + 1) * blk_cols)
        p_blk = jnp.exp(s_scr[:, cs] - m)
        denom = denom + jnp.sum(p_blk, axis=1, keepdims=True)
        p_scr[:, cs] = p_blk.astype(BF16)

    acc = _dot(p_own.astype(BF16), vn_ref[0].astype(BF16))
    for j in range(n_ck, n_stream):
        slot = j % N_SLOTS
        for cp in copies(b, j, slot):
            cp.wait()
        start_ahead(j)
        jc = j - n_ck
        acc = acc + _dot(p_scr[:, jc * cols:(jc + 1) * cols], buf[slot].reshape(cols, HEAD_DIM).astype(BF16))
        run_stages(j)
    o_ref[0] = acc / denom


def _moba_sample_and_post(pt_flat, q16, kn16, vn16, cache_k4, cache_v4, n_pages,
                          x, o_delta, o_moba, ple, w_out, ffn_norm, w_gate, w_up, w_down, w_ple, w_ple_gate, out_norm):
    nb, nq, _ = q16.shape
    t, d = x.shape
    tm = t // nb
    assert t % nb == 0 and tm % SUBLANE == 0
    page_size = cache_k4.shape[1] * cache_k4.shape[2] // N_HEADS
    n_cols = n_pages * page_size * N_HEADS
    seq = pl.BlockSpec((1, nq, HEAD_DIM), lambda i, pt: (i, 0, 0))
    tok = lambda w: pl.BlockSpec((tm, w), lambda i, pt: (i, 0))
    weights = (w_out, ffn_norm, w_gate, w_up, w_down, w_ple, w_ple_gate, out_norm)
    return pl.pallas_call(
        functools.partial(_moba_sample_kernel, n_pages=n_pages, page_size=page_size),
        grid_spec=pltpu.PrefetchScalarGridSpec(
            num_scalar_prefetch=1,
            grid=(nb,),
            in_specs=[seq, seq, seq, pl.BlockSpec(memory_space=pl.ANY), pl.BlockSpec(memory_space=pl.ANY),
                      tok(d), tok(GROUP_WIDTH), tok(GROUP_WIDTH), tok(ple.shape[1])]
                     + [_resident(w.shape) for w in weights],
            out_specs=[seq, tok(d)],
            scratch_shapes=[pltpu.VMEM((N_SLOTS, PAGES_PER_CHUNK) + cache_k4.shape[1:], F32),
                            pltpu.SemaphoreType.DMA((N_SLOTS,)),
                            pltpu.VMEM((nq, n_cols), F32),
                            pltpu.VMEM((nq, n_cols), BF16),
                            pltpu.VMEM((n_pages, 2 * N_HEADS, HEAD_DIM), F32)] + _post_scratch(tm, d)),
        out_shape=[jax.ShapeDtypeStruct((nb, nq, HEAD_DIM), F32), jax.ShapeDtypeStruct((t, d), F32)],
        compiler_params=_cparams(("arbitrary",)),
        name="moba_sample_post",
    )(pt_flat, q16, kn16, vn16, cache_k4, cache_v4, x, o_delta, o_moba, ple, *weights)


def _rope_tables(pos):
    half = HEAD_DIM // 2
    inv_freq = jnp.exp(jnp.arange(half, dtype=F32) * (-2.0 * math.log(ROPE_THETA) / HEAD_DIM))
    ang = pos.astype(F32)[:, None] * inv_freq[None, :]
    cos, sin = jnp.cos(ang), jnp.sin(ang)
    return jnp.concatenate([cos, cos], axis=-1), jnp.concatenate([-sin, sin], axis=-1)


def _pad_rows(a, n_tok):
    return jnp.pad(a, ((0, 0), (0, SUBLANE - n_tok), (0, 0)))


def _pick_tile(n, pref):
    t = min(pref, n)
    while n % t:
        t //= 2
    return t


def kernel(x_prompt, x_sample, cache_k, cache_v, page_table, state_delta, state_conv, p_prompt, p_sample,
           attn_norm, w_in, conv_w, a_log, dt_bias, delta_norm, w_out, ffn_norm, w_gate, w_up, w_down,
           w_ple, w_ple_gate, final_norm):
    bp, s_len, d = x_prompt.shape
    bs, n_tok, _ = x_sample.shape
    depth = w_in.shape[0]
    n_pages = page_table.shape[1]
    page_size = cache_k.shape[2]
    assert depth == 1, "intermediate layers would need the un-normalised hidden state"
    assert n_tok <= SUBLANE - (CONV_WIDTH - 1) and s_len % MOBA_BLOCK == 0 and s_len >= SUBLANE
    assert (n_pages * page_size) % MOBA_BLOCK == 0 and n_pages % PAGES_PER_CHUNK == 0 and page_size % 2 == 0
    assert (2 * n_pages // PAGES_PER_CHUNK) % N_SLOTS == 0
    assert n_pages * page_size // MOBA_BLOCK >= MOBA_TOPK and s_len // MOBA_BLOCK > MOBA_TOPK

    past = n_pages * page_size
    cos_p, sin_p = _rope_tables(jnp.arange(s_len, dtype=jnp.int32))
    cos_s, sin_s = _rope_tables(past + jnp.arange(n_tok, dtype=jnp.int32))
    cos_s, sin_s = jnp.tile(cos_s, (bs, 1)), jnp.tile(sin_s, (bs, 1))
    pt_flat = page_table.reshape(-1)
    row = lambda v: v.reshape(1, -1)

    hp = x_prompt.reshape(bp * s_len, d)
    hs = x_sample.reshape(bs * n_tok, d)
    tm_p = _pick_tile(s_len, 512)
    tm_s = _pick_tile(bs * n_tok, 128)
    tt = _pick_tile(s_len, SUBLANE * DELTA_CHUNK)
    outs = [[] for _ in range(8)]
    for i in range(depth):
        wi = w_in[i]
        c_dw = CONV_CH + GROUP_WIDTH
        w_in_r = jnp.concatenate(
            [wi[:, :c_dw], wi[:, c_dw + 2 * N_HEADS:], wi[:, c_dw:c_dw + 2 * N_HEADS],
             jnp.zeros((d, LANE - 2 * N_HEADS), F32)], axis=1).astype(BF16)
        alog_row = jnp.zeros((1, LANE), F32).at[0, N_HEADS:2 * N_HEADS].set(a_log[i])
        dtb_row = jnp.zeros((1, LANE), F32).at[0, N_HEADS:2 * N_HEADS].set(dt_bias[i])
        wb = lambda w: w[i].astype(BF16)
        post_w = (wb(w_out), row(ffn_norm[i]), wb(w_gate), wb(w_up), wb(w_down), wb(w_ple), wb(w_ple_gate))
        out_norm = row(final_norm)

        qkv, z, q_m, k_m, v_m, gates, k_bf, vt_bf, ksum = _in_proj(hp, row(attn_norm[i]), w_in_r, cos_p, sin_p, tm_p)
        sh = lambda a: a.reshape(bp, s_len, a.shape[-1])
        qkv3 = sh(qkv)
        hist0 = jnp.zeros((bp, CONV_WIDTH - 1, CONV_CH), F32)
        u, wk, qd, kd, qk, cd = _delta_prep(qkv3, sh(gates), hist0, conv_w[i], alog_row, dtb_row, tt)
        s00 = jnp.zeros((bp, N_HEADS, HEAD_DIM, HEAD_DIM), F32)
        o_delta, d_p = _delta_scan(u, wk, qd, kd, qk, cd, sh(z), s00, row(delta_norm[i]), tt)
        nblk = s_len // MOBA_BLOCK
        kmean = ksum.reshape(bp, s_len // tm_p, -1, GROUP_WIDTH)
        kmean = kmean.reshape(bp, -1, GROUP_WIDTH)
        if tm_p < MOBA_BLOCK:
            kmean = kmean.reshape(bp, nblk, MOBA_BLOCK // tm_p, GROUP_WIDTH).sum(axis=2)
        kmean = kmean * (1.0 / MOBA_BLOCK)
        o_moba = _moba_prompt(sh(q_m), kmean, sh(k_bf), vt_bf)
        prompt_post_in = (hp, o_delta.reshape(bp * s_len, GROUP_WIDTH), o_moba.reshape(bp * s_len, GROUP_WIDTH),
                          p_prompt[i].reshape(bp * s_len, -1))
        outs[0].append(k_m.reshape(bp, s_len, N_HEADS, HEAD_DIM))
        outs[1].append(v_m.reshape(bp, s_len, N_HEADS, HEAD_DIM))
        outs[4].append(d_p)
        outs[6].append(qkv3[:, s_len - (CONV_WIDTH - 1):])

        qkv, z, q_m, k_m, v_m, gates, _, _, _ = _in_proj(hs, row(attn_norm[i]), w_in_r, cos_s, sin_s, tm_s)
        sh = lambda a: a.reshape(bs, n_tok, a.shape[-1])
        qkv3 = sh(qkv)
        xx_full = jnp.concatenate([state_conv[i], qkv3], axis=1)
        xx = jnp.pad(xx_full, ((0, 0), (0, SUBLANE - xx_full.shape[1]), (0, 0)))
        nseq = _pick_tile(bs, 8)
        o_delta8, d_s = _delta_sample(xx, _pad_rows(sh(gates), n_tok), _pad_rows(sh(z), n_tok), state_delta[i],
                                      conv_w[i], alog_row, dtb_row, row(delta_norm[i]), n_tok, nseq)
        n_pool = cache_k.shape[1]
        page_view = (n_pool, page_size // 2, 2 * N_HEADS, HEAD_DIM)
        rows = lambda a: a.reshape(bs, n_tok * N_HEADS, HEAD_DIM)
        o_moba_s, hp = _moba_sample_and_post(pt_flat, rows(q_m), rows(k_m), rows(v_m),
                                             cache_k[i].reshape(page_view), cache_v[i].reshape(page_view), n_pages,
                                             *prompt_post_in, *post_w, out_norm)
        hs_new = _post(hs, o_delta8[:, :n_tok].reshape(bs * n_tok, GROUP_WIDTH),
                       o_moba_s.reshape(bs * n_tok, GROUP_WIDTH),
                       p_sample[i].reshape(bs * n_tok, -1), *post_w, out_norm, tm_s)
        outs[2].append(k_m.reshape(bs, n_tok, N_HEADS, HEAD_DIM))
        outs[3].append(v_m.reshape(bs, n_tok, N_HEADS, HEAD_DIM))
        outs[5].append(d_s)
        outs[7].append(xx_full[:, xx_full.shape[1] - (CONV_WIDTH - 1):])
        hs = hs_new

    y_prompt = hp.reshape(bp, s_len, d)
    y_sample = hs.reshape(bs, n_tok, d)
    st = [jnp.stack(o) for o in outs]
    return (y_prompt, y_sample, st[0], st[1], st[2], st[3], st[4], st[5], st[6], st[7])
```

```python
import functools
import math

import jax
import jax.numpy as jnp
from jax import lax
from jax.experimental import pallas as pl
from jax.experimental.pallas import tpu as pltpu

F32 = jnp.float32
BF16 = jnp.bfloat16
HI = lax.Precision.HIGHEST

HEAD_DIM = 128
N_HEADS = 4
GROUP_WIDTH = N_HEADS * HEAD_DIM
CONV_WIDTH = 4
CONV_CH = 3 * GROUP_WIDTH
DELTA_CHUNK = 64
MOBA_BLOCK = 256
MOBA_TOPK = 3
ROPE_THETA = 10000.0
RMS_EPS = 1e-6
L2_EPS = 1e-6
NEG = -1e30
LANE = 128
SUBLANE = 8
VMEM_LIMIT = 56 * 1024 * 1024

C_QKV = 0
C_Z = CONV_CH
C_QM = C_Z + GROUP_WIDTH
C_KM = C_QM + GROUP_WIDTH
C_VM = C_KM + GROUP_WIDTH
C_GATE = C_VM + GROUP_WIDTH
IN_COLS = C_GATE + LANE


def _cparams(sem):
    return pltpu.CompilerParams(dimension_semantics=sem, vmem_limit_bytes=VMEM_LIMIT)


def _resident(shape):
    nd = len(shape)
    return pl.BlockSpec(shape, lambda *_: (0,) * nd, pipeline_mode=pl.Buffered(1))


def _sigmoid(x):
    return 1.0 / (1.0 + jnp.exp(-x))


def _silu(x):
    return x * _sigmoid(x)


def _softplus(x):
    return jnp.maximum(x, 0.0) + jnp.log(1.0 + jnp.exp(-jnp.abs(x)))


def _rms(x, g):
    return x * lax.rsqrt(jnp.mean(x * x, axis=-1, keepdims=True) + RMS_EPS) * g


def _dot(a, b, prec=None):
    return jnp.dot(a, b, precision=prec, preferred_element_type=F32)


def _dot_nt(a, b, prec=None):
    return lax.dot_general(a, b, (((1,), (1,)), ((), ())), precision=prec, preferred_element_type=F32)


def _bmm(a, b, prec=None):
    return lax.dot_general(a, b, (((2,), (1,)), ((0,), (0,))), precision=prec, preferred_element_type=F32)


def _bmm_nt(a, b, prec=None):
    return lax.dot_general(a, b, (((2,), (2,)), ((0,), (0,))), precision=prec, preferred_element_type=F32)


def _bmm_tn(a, b, prec=None):
    return lax.dot_general(a, b, (((1,), (1,)), ((0,), (0,))), precision=prec, preferred_element_type=F32)


def _in_proj_kernel(x_ref, g_ref, w_ref, cos_ref, sin_ref,
                    qkv_ref, z_ref, q_ref, k_ref, v_ref, gate_ref, kbf_ref, vt_ref, ksum_ref):
    a = _rms(x_ref[...], g_ref[...]).astype(BF16)
    cos = cos_ref[...]
    sin = sin_ref[...]

    def proj(c0, width):
        return _dot(a, w_ref[:, c0:c0 + width])

    def rope(t):
        heads = []
        for h in range(N_HEADS):
            th = t[:, h * HEAD_DIM:(h + 1) * HEAD_DIM]
            heads.append(th * cos + pltpu.roll(th, HEAD_DIM // 2, axis=1) * sin)
        return jnp.concatenate(heads, axis=1)

    qkv_ref[...] = proj(C_QKV, CONV_CH)
    z_ref[...] = proj(C_Z, GROUP_WIDTH)
    q_ref[...] = rope(proj(C_QM, GROUP_WIDTH))
    k = rope(proj(C_KM, GROUP_WIDTH))
    v = proj(C_VM, GROUP_WIDTH)
    tm = k.shape[0]
    for h in range(N_HEADS):
        k_ref[pl.ds(h, tm, stride=N_HEADS), :] = k[:, h * HEAD_DIM:(h + 1) * HEAD_DIM]
        v_ref[pl.ds(h, tm, stride=N_HEADS), :] = v[:, h * HEAD_DIM:(h + 1) * HEAD_DIM]
    kbf_ref[...] = k.astype(BF16)
    vt_ref[0] = v.T.astype(BF16)
    nblk = ksum_ref.shape[1]
    ksum_ref[0] = jnp.sum(k.reshape(nblk, k.shape[0] // nblk, GROUP_WIDTH), axis=1)
    gate_ref[...] = proj(C_GATE, LANE)


def _in_proj(x, norm_w, w_in_r, cos, sin, tm):
    t, d = x.shape
    nt = t // tm
    tiles_per_seq = cos.shape[0] // tm
    n_tab = tiles_per_seq
    nblk = max(tm // MOBA_BLOCK, 1)
    tok = lambda w: pl.BlockSpec((tm, w), lambda i: (i, 0))
    tab = pl.BlockSpec((tm, HEAD_DIM), lambda i: (i % n_tab, 0))
    out_shapes = [
        jax.ShapeDtypeStruct((t, CONV_CH), F32), jax.ShapeDtypeStruct((t, GROUP_WIDTH), F32),
        jax.ShapeDtypeStruct((t, GROUP_WIDTH), F32), jax.ShapeDtypeStruct((t * N_HEADS, HEAD_DIM), F32),
        jax.ShapeDtypeStruct((t * N_HEADS, HEAD_DIM), F32), jax.ShapeDtypeStruct((t, LANE), F32),
        jax.ShapeDtypeStruct((t, GROUP_WIDTH), BF16),
        jax.ShapeDtypeStruct((nt // tiles_per_seq, GROUP_WIDTH, tiles_per_seq * tm), BF16),
        jax.ShapeDtypeStruct((nt, nblk, GROUP_WIDTH), F32),
    ]
    head_rows = pl.BlockSpec((tm * N_HEADS, HEAD_DIM), lambda i: (i, 0))
    out_specs = [tok(CONV_CH), tok(GROUP_WIDTH), tok(GROUP_WIDTH), head_rows, head_rows,
                 tok(LANE), tok(GROUP_WIDTH),
                 pl.BlockSpec((1, GROUP_WIDTH, tm), lambda i: (i // tiles_per_seq, 0, i % tiles_per_seq)),
                 pl.BlockSpec((1, nblk, GROUP_WIDTH), lambda i: (i, 0, 0))]
    return pl.pallas_call(
        _in_proj_kernel,
        grid=(nt,),
        in_specs=[tok(d), _resident((1, d)), _resident(w_in_r.shape), tab, tab],
        out_specs=out_specs,
        out_shape=out_shapes,
        compiler_params=_cparams(("arbitrary",)),
        name="in_proj",
    )(x, norm_w, w_in_r, cos, sin)


def _split(x):
    hi = x.astype(BF16)
    return hi, (x - hi.astype(F32)).astype(BF16)


def _bmm_split(a, b, nt=False):
    ah, al = _split(a)
    bh, bl = _split(b)
    lhs = jnp.concatenate([ah, al, ah], axis=2)
    if nt:
        return _bmm_nt(lhs, jnp.concatenate([bh, bh, bl], axis=2))
    return _bmm(lhs, jnp.concatenate([bh, bh, bl], axis=1))


def _dot_nt_split(a, b):
    ah, al = _split(a)
    bh, bl = _split(b)
    return _dot_nt(jnp.concatenate([ah, al, ah], axis=1), jnp.concatenate([bh, bh, bl], axis=1))


def _bmm_exact(a, b, nt=False):
    return _bmm_nt(a, b, HI) if nt else _bmm(a, b, HI)


def _chunk_prep(q, k, v, g, beta, mm):
    n, c, dh = q.shape
    row = lax.broadcasted_iota(jnp.int32, (c, c), 0)
    col = lax.broadcasted_iota(jnp.int32, (c, c), 1)
    causal = row >= col
    strict = row > col
    gx = jnp.concatenate([jnp.broadcast_to(g, (n, c, dh)),
                          jnp.where(strict, jnp.broadcast_to(g, (n, c, c)), 0.0)], axis=2)
    g1 = gx.astype(BF16)
    r1 = gx - g1.astype(F32)
    g2 = r1.astype(BF16)
    g3 = (r1 - g2.astype(F32)).astype(BF16)
    lower3 = jnp.broadcast_to(jnp.concatenate([causal.astype(BF16)] * 3, axis=1), (n, c, 3 * c))
    csum = _bmm(lower3, jnp.concatenate([g1, g2, g3], axis=1))
    gcb = csum[:, :, :dh]
    dlog = csum[:, :, dh:]
    decay = jnp.where(causal, jnp.exp(jnp.where(causal, dlog, 0.0)), 0.0)
    kb = k * beta
    kq = mm(jnp.concatenate([kb, q], axis=1), k, nt=True)
    a_mat = jnp.where(strict, kq[:, :c] * decay, 0.0)
    qk = jnp.where(causal, kq[:, c:] * decay, 0.0)
    nmat = -a_mat
    t_inv = jnp.where(row == col, 1.0, 0.0) + nmat
    n_factors = int(math.log2(c))
    if n_factors > 1:
        nmat = mm(nmat, nmat)
    for i in range(1, n_factors):
        if i + 1 < n_factors:
            both = mm(jnp.concatenate([nmat, t_inv], axis=1), nmat)
            nmat, t_inv = both[:, :c], t_inv + both[:, c:]
        else:
            t_inv = t_inv + mm(t_inv, nmat)
    egc = jnp.exp(gcb)
    uw = mm(t_inv, jnp.concatenate([v * beta, kb * egc], axis=2))
    g_last = gcb[:, c - 1:c, :]
    return uw[:, :, :dh], uw[:, :, dh:], qk, q * egc, k * jnp.exp(g_last - gcb), jnp.exp(g_last)


def _chunk_apply(s, u, wk, qk, qd, kd, cd):
    c = u.shape[1]
    xs = _bmm(jnp.concatenate([wk, qd], axis=1).astype(BF16), s.astype(BF16))
    w = u - xs[:, :c]
    wb = w.astype(BF16)
    o = xs[:, c:] + _bmm(qk.astype(BF16), wb)
    s_new = s * cd + _bmm_tn(kd.astype(BF16), wb)
    return o, s_new


def _delta_inputs(y, gates, alog, dtb, h):
    sl = lambda base: slice(base + h * HEAD_DIM, base + (h + 1) * HEAD_DIM)
    q = y[:, sl(0)]
    k = y[:, sl(GROUP_WIDTH)]
    v = y[:, sl(2 * GROUP_WIDTH)]
    q = q * lax.rsqrt(jnp.sum(q * q, axis=-1, keepdims=True) + L2_EPS) * (HEAD_DIM ** -0.5)
    k = k * lax.rsqrt(jnp.sum(k * k, axis=-1, keepdims=True) + L2_EPS)
    beta = _sigmoid(gates[:, h:h + 1])
    g = -jnp.exp(alog[:, N_HEADS + h:N_HEADS + h + 1]) * _softplus(
        gates[:, N_HEADS + h:N_HEADS + h + 1] + dtb[:, N_HEADS + h:N_HEADS + h + 1])
    return q, k, v, g, beta


def _delta_prep_kernel(qkv_ref, gate_ref, hist_ref, convw_ref, alog_ref, dtb_ref,
                       u_ref, wk_ref, qd_ref, kd_ref, qk_ref, cd_ref, prev_ref):
    tt = qkv_ref.shape[1]
    n = tt // DELTA_CHUNK

    @pl.when(pl.program_id(1) == 0)
    def _():
        prev_ref[...] = jnp.zeros_like(prev_ref)
        prev_ref[SUBLANE - (CONV_WIDTH - 1):, :] = hist_ref[0]

    x = qkv_ref[0]
    prev = prev_ref[...]
    w = convw_ref[...]
    first_rows = lax.broadcasted_iota(jnp.int32, (SUBLANE, 1), 0)
    y = x * w[CONV_WIDTH - 1:CONV_WIDTH]
    for back in range(1, CONV_WIDTH):
        shifted = pltpu.roll(x, back, axis=0)
        head = jnp.where(first_rows < back, pltpu.roll(prev, back, axis=0), shifted[:SUBLANE])
        shifted = jnp.concatenate([head, shifted[SUBLANE:]], axis=0)
        y = y + shifted * w[CONV_WIDTH - 1 - back:CONV_WIDTH - back]
    y = _silu(y)
    prev_ref[...] = x[tt - SUBLANE:]

    gates = gate_ref[0]
    alog = alog_ref[...]
    dtb = dtb_ref[...]
    for h in range(N_HEADS):
        q, k, v, g, beta = _delta_inputs(y, gates, alog, dtb, h)
        r3 = lambda t: t.reshape(n, DELTA_CHUNK, t.shape[-1])
        u, wk, qk, qd, kd, cd = _chunk_prep(r3(q), r3(k), r3(v), r3(g), r3(beta), _bmm_split)
        hs = slice(h * HEAD_DIM, (h + 1) * HEAD_DIM)
        u_ref[0, :, hs] = u.reshape(tt, HEAD_DIM)
        wk_ref[0, :, hs] = wk.reshape(tt, HEAD_DIM)
        qd_ref[0, :, hs] = qd.reshape(tt, HEAD_DIM)
        kd_ref[0, :, hs] = kd.reshape(tt, HEAD_DIM)
        qk_ref[0, :, h * DELTA_CHUNK:(h + 1) * DELTA_CHUNK] = qk.reshape(tt, DELTA_CHUNK)
        cd_ref[0, :, hs] = cd.reshape(n, HEAD_DIM)


def _delta_prep(qkv, gates, hist, conv_w, alog_row, dtb_row, tt):
    b, s, _ = qkv.shape
    n = tt // DELTA_CHUNK
    tok = lambda w: pl.BlockSpec((1, tt, w), lambda bi, j: (bi, j, 0))
    wide = jax.ShapeDtypeStruct((b, s, GROUP_WIDTH), F32)
    return pl.pallas_call(
        _delta_prep_kernel,
        grid=(b, s // tt),
        in_specs=[tok(CONV_CH), tok(LANE),
                  pl.BlockSpec((1, CONV_WIDTH - 1, CONV_CH), lambda bi, j: (bi, 0, 0)),
                  _resident(conv_w.shape), _resident((1, LANE)), _resident((1, LANE))],
        out_specs=[tok(GROUP_WIDTH), tok(GROUP_WIDTH), tok(GROUP_WIDTH), tok(GROUP_WIDTH),
                   tok(N_HEADS * DELTA_CHUNK),
                   pl.BlockSpec((1, n, GROUP_WIDTH), lambda bi, j: (bi, j, 0))],
        out_shape=[wide, wide, wide, wide,
                   jax.ShapeDtypeStruct((b, s, N_HEADS * DELTA_CHUNK), F32),
                   jax.ShapeDtypeStruct((b, s // DELTA_CHUNK, GROUP_WIDTH), F32)],
        scratch_shapes=[pltpu.VMEM((SUBLANE, CONV_CH), F32)],
        compiler_params=_cparams(("arbitrary", "arbitrary")),
        name="delta_prep",
    )(qkv, gates, hist, conv_w, alog_row, dtb_row)


def _delta_scan_kernel(u_ref, wk_ref, qd_ref, kd_ref, qk_ref, cd_ref, z_ref, s0_ref, dnorm_ref,
                       o_ref, sfin_ref, s_scr):
    nb, tt, _ = u_ref.shape
    n = tt // DELTA_CHUNK

    @pl.when(pl.program_id(0) == 0)
    def _():
        s_scr[...] = s0_ref[...]

    dnorm = dnorm_ref[...]
    for ci in range(n):
        rows = slice(ci * DELTA_CHUNK, (ci + 1) * DELTA_CHUNK)
        for b in range(nb):
            for h in range(N_HEADS):
                hs = slice(h * HEAD_DIM, (h + 1) * HEAD_DIM)
                o, s_new = _chunk_apply(
                    s_scr[b, h][None], u_ref[b, rows, hs][None], wk_ref[b, rows, hs][None],
                    qk_ref[b, rows, h * DELTA_CHUNK:(h + 1) * DELTA_CHUNK][None],
                    qd_ref[b, rows, hs][None], kd_ref[b, rows, hs][None], cd_ref[b, ci:ci + 1, hs][None])
                s_scr[b, h] = s_new[0]
                o_ref[b, rows, hs] = _rms(o[0], dnorm) * _silu(z_ref[b, rows, hs])

    @pl.when(pl.program_id(0) == pl.num_programs(0) - 1)
    def _():
        sfin_ref[...] = s_scr[...]


def _delta_scan(u, wk, qd, kd, qk, cd, z, s0, dnorm, tt):
    b, s, _ = u.shape
    n = tt // DELTA_CHUNK
    tok = lambda w: pl.BlockSpec((b, tt, w), lambda j: (0, j, 0))
    st = pl.BlockSpec(s0.shape, lambda j: (0, 0, 0, 0))
    return pl.pallas_call(
        _delta_scan_kernel,
        grid=(s // tt,),
        in_specs=[tok(GROUP_WIDTH), tok(GROUP_WIDTH), tok(GROUP_WIDTH), tok(GROUP_WIDTH),
                  tok(N_HEADS * DELTA_CHUNK), pl.BlockSpec((b, n, GROUP_WIDTH), lambda j: (0, j, 0)),
                  tok(GROUP_WIDTH), st, _resident((1, HEAD_DIM))],
        out_specs=[tok(GROUP_WIDTH), st],
        out_shape=[jax.ShapeDtypeStruct((b, s, GROUP_WIDTH), F32), jax.ShapeDtypeStruct(s0.shape, F32)],
        scratch_shapes=[pltpu.VMEM(s0.shape, F32)],
        compiler_params=_cparams(("arbitrary",)),
        name="delta_scan",
    )(u, wk, qd, kd, qk, cd, z, s0, dnorm)


def _topk_blocks(gate, valid, axis):
    nb = gate.shape[axis]
    blk = lax.broadcasted_iota(jnp.int32, gate.shape, axis).astype(F32)
    gv = jnp.where(valid, gate, -jnp.inf)
    hits = []
    for _ in range(MOBA_TOPK):
        top = jnp.max(gv, axis=axis, keepdims=True)
        first = jnp.min(jnp.where(gv == top, blk, float(nb)), axis=axis, keepdims=True)
        hit = (blk == first) & (top > -jnp.inf)
        gv = jnp.where(hit, -jnp.inf, gv)
        hits.append(hit)
    return hits


def _moba_prompt_kernel(q_ref, kmean_ref, k_ref, vt_ref, o_ref,
                        bias_scr, qbf_scr, m_scr, l_scr, acc_scr, sa_scr, sb_scr):
    cur = pl.program_id(1)
    tq = q_ref.shape[1]
    nb = kmean_ref.shape[1]
    scale = HEAD_DIM ** -0.5
    key_i = lax.broadcasted_iota(jnp.int32, (MOBA_BLOCK, tq), 0)
    qry_i = lax.broadcasted_iota(jnp.int32, (MOBA_BLOCK, tq), 1)
    causal_bias = jnp.where(key_i <= qry_i, 0.0, NEG)
    blk_ids = lax.broadcasted_iota(jnp.int32, (nb, tq), 0)
    heads = [slice(h * HEAD_DIM, (h + 1) * HEAD_DIM) for h in range(N_HEADS)]

    for h, hs in enumerate(heads):
        q = q_ref[0, :, hs]
        gate_t = _dot_nt_split(kmean_ref[0, :, hs], q)
        chosen = jnp.zeros((nb, tq), jnp.bool_)
        for hit in _topk_blocks(gate_t, blk_ids < cur, 0):
            chosen = chosen | hit
        bias_scr[h] = jnp.where(chosen, 0.0, NEG)
        qbf_scr[h] = (q * scale).astype(BF16)
        m_scr[h] = jnp.full((1, tq), NEG, F32)
        l_scr[h] = jnp.zeros((1, tq), F32)
        acc_scr[h] = jnp.zeros((HEAD_DIM, tq), F32)

    def scores(jb, s_buf):
        r0 = pl.multiple_of(jb * MOBA_BLOCK, MOBA_BLOCK)
        for h, hs in enumerate(heads):
            s_buf[h] = _dot_nt(k_ref[0, pl.ds(r0, MOBA_BLOCK), hs], qbf_scr[h])

    def update(jb, s_buf, bias_of):
        r0 = pl.multiple_of(jb * MOBA_BLOCK, MOBA_BLOCK)
        for h, hs in enumerate(heads):
            s = s_buf[h] + bias_of(h)
            m_i = m_scr[h]
            m_new = jnp.maximum(m_i, jnp.max(s, axis=0, keepdims=True))
            alpha = jnp.exp(m_i - m_new)
            p = jnp.exp(s - m_new)
            m_scr[h] = m_new
            l_scr[h] = alpha * l_scr[h] + jnp.sum(p, axis=0, keepdims=True)
            acc_scr[h] = alpha * acc_scr[h] + _dot(vt_ref[0, hs, pl.ds(r0, MOBA_BLOCK)], p.astype(BF16))

    def chosen_bias(jb):
        return lambda h: bias_scr[h, pl.ds(jb, 1), :]

    def own_bias(h):
        return causal_bias

    scores(0, sa_scr)

    def block_pair(pair, carry):
        jb = 2 * pair
        scores(jb + 1, sb_scr)
        update(jb, sa_scr, chosen_bias(jb))
        scores(jb + 2, sa_scr)
        update(jb + 1, sb_scr, chosen_bias(jb + 1))
        return carry

    lax.fori_loop(0, cur // 2, block_pair, 0)

    @pl.when(cur % 2 == 1)
    def _():
        scores(cur, sb_scr)
        update(cur - 1, sa_scr, chosen_bias(cur - 1))
        update(cur, sb_scr, own_bias)

    @pl.when(cur % 2 == 0)
    def _():
        update(cur, sa_scr, own_bias)

    for h, hs in enumerate(heads):
        o_ref[0, :, hs] = (acc_scr[h] / l_scr[h]).T


def _moba_prompt(q, kmean, k_bf, vt_bf):
    b, s, _ = q.shape
    nb = s // MOBA_BLOCK
    tok = pl.BlockSpec((1, MOBA_BLOCK, GROUP_WIDTH), lambda bi, j: (bi, j, 0))
    seq = lambda a: pl.BlockSpec((1,) + a.shape[1:], lambda bi, j: (bi, 0, 0))
    return pl.pallas_call(
        _moba_prompt_kernel,
        grid=(b, nb),
        in_specs=[tok, seq(kmean), seq(k_bf), seq(vt_bf)],
        out_specs=tok,
        out_shape=jax.ShapeDtypeStruct((b, s, GROUP_WIDTH), F32),
        scratch_shapes=[pltpu.VMEM((N_HEADS, nb, MOBA_BLOCK), F32),
                        pltpu.VMEM((N_HEADS, MOBA_BLOCK, HEAD_DIM), BF16),
                        pltpu.VMEM((N_HEADS, 1, MOBA_BLOCK), F32),
                        pltpu.VMEM((N_HEADS, 1, MOBA_BLOCK), F32),
                        pltpu.VMEM((N_HEADS, HEAD_DIM, MOBA_BLOCK), F32),
                        pltpu.VMEM((N_HEADS, MOBA_BLOCK, MOBA_BLOCK), F32),
                        pltpu.VMEM((N_HEADS, MOBA_BLOCK, MOBA_BLOCK), F32)],
        compiler_params=_cparams(("arbitrary", "arbitrary")),
        name="moba_prompt",
    )(q, kmean, k_bf, vt_bf)


FF_CHUNK = 256
N_POST_STAGES = 8


def _post_stages(x_ref, od_ref, om_ref, ple_ref, wout_ref, fnorm_ref, wg_ref, wu_ref, wd_ref,
                 wple_ref, wpg_ref, onorm_ref, y_ref, h_scr, f_scr, ffn_scr):
    d_ff = wg_ref.shape[1]
    ff = FF_CHUNK if d_ff % FF_CHUNK == 0 else d_ff
    starts = list(range(0, d_ff, ff))
    n_mid = N_POST_STAGES - 2
    groups = [starts[i * len(starts) // n_mid:(i + 1) * len(starts) // n_mid] for i in range(n_mid)]

    def head():
        mix = jnp.concatenate([od_ref[...], om_ref[...]], axis=1).astype(BF16)
        h = x_ref[...] + _dot(mix, wout_ref[...])
        h_scr[...] = h
        f_scr[...] = _rms(h, fnorm_ref[...]).astype(BF16)
        ffn_scr[...] = jnp.zeros_like(ffn_scr)

    def ffn(group):
        def run():
            f = f_scr[...]
            part = None
            for c0 in group:
                gate = _dot(f, wg_ref[:, c0:c0 + ff])
                up = _dot(f, wu_ref[:, c0:c0 + ff])
                down = _dot((_silu(gate) * up).astype(BF16), wd_ref[c0:c0 + ff, :])
                part = down if part is None else part + down
            if part is not None:
                ffn_scr[...] += part
        return run

    def tail():
        h = h_scr[...] + ffn_scr[...]
        h = h + _dot(ple_ref[...].astype(BF16), wple_ref[...]) * _sigmoid(_dot(h.astype(BF16), wpg_ref[...]))
        y_ref[...] = _rms(h, onorm_ref[...])

    return [head] + [ffn(g) for g in groups] + [tail]


def _post_scratch(tm, d):
    return [pltpu.VMEM((tm, d), F32), pltpu.VMEM((tm, d), BF16), pltpu.VMEM((tm, d), F32)]


def _post_kernel(*refs):
    for stage in _post_stages(*refs):
        stage()


def _post(x, o_delta, o_moba, ple, w_out, ffn_norm, w_gate, w_up, w_down, w_ple, w_ple_gate, out_norm, tm):
    t, d = x.shape
    tok = lambda w: pl.BlockSpec((tm, w), lambda i: (i, 0))
    return pl.pallas_call(
        _post_kernel,
        grid=(t // tm,),
        in_specs=[tok(d), tok(GROUP_WIDTH), tok(GROUP_WIDTH), tok(ple.shape[1]),
                  _resident(w_out.shape), _resident((1, d)), _resident(w_gate.shape), _resident(w_up.shape),
                  _resident(w_down.shape), _resident(w_ple.shape), _resident(w_ple_gate.shape), _resident((1, d))],
        out_specs=tok(d),
        out_shape=jax.ShapeDtypeStruct((t, d), F32),
        scratch_shapes=_post_scratch(tm, d),
        compiler_params=_cparams(("arbitrary",)),
        name="post",
    )(x, o_delta, o_moba, ple, w_out, ffn_norm, w_gate, w_up, w_down, w_ple, w_ple_gate, out_norm)


def _delta_sample_kernel(xx_ref, gate_ref, z_ref, s_ref, convw_ref, alog_ref, dtb_ref, dnorm_ref,
                         o_ref, snew_ref, *, n_tok):
    nseq = xx_ref.shape[0]
    rows = nseq * SUBLANE
    xx = xx_ref[...].reshape(rows, CONV_CH)
    w = convw_ref[...]
    y = xx * w[0:1]
    for j in range(1, CONV_WIDTH):
        y = y + pltpu.roll(xx, rows - j, axis=0) * w[j:j + 1]
    y = _silu(y)
    gates = gate_ref[...].reshape(rows, LANE)
    real = (lax.broadcasted_iota(jnp.int32, (rows, 1), 0) % SUBLANE) < n_tok
    dnorm = dnorm_ref[...]
    for h in range(N_HEADS):
        q, k, v, g, beta = _delta_inputs(y, gates, alog_ref[...], dtb_ref[...], h)
        g = jnp.where(real, g, 0.0)
        beta = jnp.where(real, beta, 0.0)
        r3 = lambda t: t.reshape(nseq, SUBLANE, t.shape[-1])
        o, s_new = _chunk_apply(s_ref[:, h], *_chunk_prep(r3(q), r3(k), r3(v), r3(g), r3(beta), _bmm_exact))
        snew_ref[:, h] = s_new
        hs = slice(h * HEAD_DIM, (h + 1) * HEAD_DIM)
        o_ref[:, :, hs] = _rms(o, dnorm) * _silu(z_ref[:, :, hs])


def _delta_sample(xx, gates8, z8, state, conv_w, alog_row, dtb_row, dnorm, n_tok, nseq):
    nb = xx.shape[0]
    seq = lambda w: pl.BlockSpec((nseq, SUBLANE, w), lambda i: (i, 0, 0))
    st = pl.BlockSpec((nseq,) + state.shape[1:], lambda i: (i, 0, 0, 0))
    return pl.pallas_call(
        functools.partial(_delta_sample_kernel, n_tok=n_tok),
        grid=(nb // nseq,),
        in_specs=[seq(CONV_CH), seq(LANE), seq(GROUP_WIDTH), st, _resident(conv_w.shape),
                  _resident((1, LANE)), _resident((1, LANE)), _resident((1, HEAD_DIM))],
        out_specs=[seq(GROUP_WIDTH), st],
        out_shape=[jax.ShapeDtypeStruct((nb, SUBLANE, GROUP_WIDTH), F32), jax.ShapeDtypeStruct(state.shape, F32)],
        compiler_params=_cparams(("arbitrary",)),
        name="delta_sample",
    )(xx, gates8, z8, state, conv_w, alog_row, dtb_row, dnorm)


PAGES_PER_CHUNK = 16
N_SLOTS = 4


def _moba_sample_kernel(pt_ref, q_ref, kn_ref, vn_ref, kc_ref, vc_ref, *rest, n_pages, page_size):
    post_in, (o_ref, y_ref), (buf, sem, s_scr, p_scr, psum, h_scr, f_scr, ffn_scr) = rest[:12], rest[12:14], rest[14:]
    stages = _post_stages(*post_in, y_ref, h_scr, f_scr, ffn_scr)
    b = pl.program_id(0)
    nseq = pl.num_programs(0)
    n_ck = n_pages // PAGES_PER_CHUNK
    n_stream = 2 * n_ck
    rows_per_page = page_size * N_HEADS
    cols = PAGES_PER_CHUNK * rows_per_page
    blk_cols = MOBA_BLOCK * N_HEADS
    n_blocks = n_pages * page_size // MOBA_BLOCK
    nq = q_ref.shape[1]
    scale = HEAD_DIM ** -0.5

    def copies(seq, j, slot):
        src = kc_ref if j < n_ck else vc_ref
        first = seq * n_pages + (j % n_ck) * PAGES_PER_CHUNK
        return [pltpu.make_async_copy(src.at[pt_ref[first + i]], buf.at[slot, i], sem.at[slot])
                for i in range(PAGES_PER_CHUNK)]

    def start_ahead(j):
        ahead = j + N_SLOTS - 1
        if ahead < n_stream:
            for cp in copies(b, ahead, ahead % N_SLOTS):
                cp.start()
        else:
            @pl.when(b + 1 < nseq)
            def _():
                for cp in copies(b + 1, ahead - n_stream, ahead % N_SLOTS):
                    cp.start()

    @pl.when(b == 0)
    def _():
        for j in range(N_SLOTS - 1):
            for cp in copies(0, j, j):
                cp.start()

    def run_stages(j):
        for stage in stages[j * len(stages) // n_stream:(j + 1) * len(stages) // n_stream]:
            stage()

    q = q_ref[0]
    q_bf = q.astype(BF16)
    head_ok = (lax.broadcasted_iota(jnp.int32, (nq, blk_cols), 0) % N_HEADS
               == lax.broadcasted_iota(jnp.int32, (nq, blk_cols), 1) % N_HEADS)
    blk_lane = lax.broadcasted_iota(jnp.int32, (nq, n_blocks), 1)
    blocks_per_chunk = cols // blk_cols

    blk_max = jnp.full((nq, n_blocks), NEG, F32)
    for j in range(n_ck):
        slot = j % N_SLOTS
        for cp in copies(b, j, slot):
            cp.wait()
        start_ahead(j)
        pages = buf[slot]
        psum[j * PAGES_PER_CHUNK:(j + 1) * PAGES_PER_CHUNK] = jnp.sum(pages, axis=1)
        s_chunk = _dot_nt(q_bf, pages.reshape(cols, HEAD_DIM).astype(BF16)) * scale
        s_scr[:, j * cols:(j + 1) * cols] = s_chunk
        for i in range(blocks_per_chunk):
            top = jnp.max(jnp.where(head_ok, s_chunk[:, i * blk_cols:(i + 1) * blk_cols], NEG), axis=1, keepdims=True)
            blk_max = jnp.where(blk_lane == j * blocks_per_chunk + i, top, blk_max)
        run_stages(j)

    prow = n_pages * 2 * N_HEADS
    gate_rows = _dot_nt(q, psum[...].reshape(prow, HEAD_DIM), HI)
    r_head = lax.broadcasted_iota(jnp.int32, (nq, prow), 0) % N_HEADS
    c_head = lax.broadcasted_iota(jnp.int32, (nq, prow), 1) % N_HEADS
    gate_rows = jnp.where(r_head == c_head, gate_rows, 0.0)
    rows_per_block = prow // n_blocks
    fold = jnp.where(lax.broadcasted_iota(jnp.int32, (prow, n_blocks), 0) // rows_per_block
                     == lax.broadcasted_iota(jnp.int32, (prow, n_blocks), 1), 1.0, 0.0)
    gate = _dot(gate_rows, fold, HI) * (1.0 / MOBA_BLOCK)
    sel = jnp.zeros((nq, n_blocks), F32)
    for hit in _topk_blocks(gate, jnp.full((nq, n_blocks), True), 1):
        sel = jnp.where(hit, 1.0, sel)

    orow = lax.broadcasted_iota(jnp.int32, (nq, nq), 0)
    ocol = lax.broadcasted_iota(jnp.int32, (nq, nq), 1)
    own_ok = (orow % N_HEADS == ocol % N_HEADS) & (ocol // N_HEADS <= orow // N_HEADS)
    s_own = jnp.where(own_ok, _dot_nt(q_bf, kn_ref[0].astype(BF16)) * scale, NEG)
    m = jnp.maximum(jnp.max(s_own, axis=1, keepdims=True),
                    jnp.max(jnp.where(sel > 0.5, blk_max, NEG), axis=1, keepdims=True))
    p_own = jnp.exp(s_own - m)
    denom = jnp.sum(p_own, axis=1, keepdims=True)
    for n in range(n_blocks):
        cs = slice(n * blk_cols, (n + 1) * blk_cols)
        p_blk = jnp.where(head_ok & (sel[:, n:n + 1] > 0.5), jnp.exp(s_scr[:, cs] - m), 0.0)
        denom = denom + jnp.sum(p_blk, axis=1, keepdims=True)
        p_scr[:, cs] = p_blk.astype(BF16)

    acc = _dot(p_own.astype(BF16), vn_ref[0].astype(BF16))
    for j in range(n_ck, n_stream):
        slot = j % N_SLOTS
        for cp in copies(b, j, slot):
            cp.wait()
        start_ahead(j)
        jc = j - n_ck
        acc = acc + _dot(p_scr[:, jc * cols:(jc + 1) * cols], buf[slot].reshape(cols, HEAD_DIM).astype(BF16))
        run_stages(j)
    o_ref[0] = acc / denom


def _moba_sample_and_post(pt_flat, q16, kn16, vn16, cache_k4, cache_v4, n_pages,
                          x, o_delta, o_moba, ple, w_out, ffn_norm, w_gate, w_up, w_down, w_ple, w_ple_gate, out_norm):
    nb, nq, _ = q16.shape
    t, d = x.shape
    tm = t // nb
    assert t % nb == 0 and tm % SUBLANE == 0
    page_size = cache_k4.shape[1] * cache_k4.shape[2] // N_HEADS
    n_cols = n_pages * page_size * N_HEADS
    seq = pl.BlockSpec((1, nq, HEAD_DIM), lambda i, pt: (i, 0, 0))
    tok = lambda w: pl.BlockSpec((tm, w), lambda i, pt: (i, 0))
    weights = (w_out, ffn_norm, w_gate, w_up, w_down, w_ple, w_ple_gate, out_norm)
    return pl.pallas_call(
        functools.partial(_moba_sample_kernel, n_pages=n_pages, page_size=page_size),
        grid_spec=pltpu.PrefetchScalarGridSpec(
            num_scalar_prefetch=1,
            grid=(nb,),
            in_specs=[seq, seq, seq, pl.BlockSpec(memory_space=pl.ANY), pl.BlockSpec(memory_space=pl.ANY),
                      tok(d), tok(GROUP_WIDTH), tok(GROUP_WIDTH), tok(ple.shape[1])]
                     + [_resident(w.shape) for w in weights],
            out_specs=[seq, tok(d)],
            scratch_shapes=[pltpu.VMEM((N_SLOTS, PAGES_PER_CHUNK) + cache_k4.shape[1:], F32),
                            pltpu.SemaphoreType.DMA((N_SLOTS,)),
                            pltpu.VMEM((nq, n_cols), F32),
                            pltpu.VMEM((nq, n_cols), BF16),
                            pltpu.VMEM((n_pages, 2 * N_HEADS, HEAD_DIM), F32)] + _post_scratch(tm, d)),
        out_shape=[jax.ShapeDtypeStruct((nb, nq, HEAD_DIM), F32), jax.ShapeDtypeStruct((t, d), F32)],
        compiler_params=_cparams(("arbitrary",)),
        name="moba_sample_post",
    )(pt_flat, q16, kn16, vn16, cache_k4, cache_v4, x, o_delta, o_moba, ple, *weights)


def _rope_tables(pos):
    half = HEAD_DIM // 2
    inv_freq = jnp.exp(jnp.arange(half, dtype=F32) * (-2.0 * math.log(ROPE_THETA) / HEAD_DIM))
    ang = pos.astype(F32)[:, None] * inv_freq[None, :]
    cos, sin = jnp.cos(ang), jnp.sin(ang)
    return jnp.concatenate([cos, cos], axis=-1), jnp.concatenate([-sin, sin], axis=-1)


def _pad_rows(a, n_tok):
    return jnp.pad(a, ((0, 0), (0, SUBLANE - n_tok), (0, 0)))


def _pick_tile(n, pref):
    t = min(pref, n)
    while n % t:
        t //= 2
    return t


def kernel(x_prompt, x_sample, cache_k, cache_v, page_table, state_delta, state_conv, p_prompt, p_sample,
           attn_norm, w_in, conv_w, a_log, dt_bias, delta_norm, w_out, ffn_norm, w_gate, w_up, w_down,
           w_ple, w_ple_gate, final_norm):
    bp, s_len, d = x_prompt.shape
    bs, n_tok, _ = x_sample.shape
    depth = w_in.shape[0]
    n_pages = page_table.shape[1]
    page_size = cache_k.shape[2]
    assert depth == 1, "intermediate layers would need the un-normalised hidden state"
    assert n_tok <= SUBLANE - (CONV_WIDTH - 1) and s_len % MOBA_BLOCK == 0 and s_len >= SUBLANE
    assert (n_pages * page_size) % MOBA_BLOCK == 0 and n_pages % PAGES_PER_CHUNK == 0 and page_size % 2 == 0
    assert (2 * n_pages // PAGES_PER_CHUNK) % N_SLOTS == 0
    assert n_pages * page_size // MOBA_BLOCK >= MOBA_TOPK and s_len // MOBA_BLOCK > MOBA_TOPK

    past = n_pages * page_size
    cos_p, sin_p = _rope_tables(jnp.arange(s_len, dtype=jnp.int32))
    cos_s, sin_s = _rope_tables(past + jnp.arange(n_tok, dtype=jnp.int32))
    cos_s, sin_s = jnp.tile(cos_s, (bs, 1)), jnp.tile(sin_s, (bs, 1))
    pt_flat = page_table.reshape(-1)
    row = lambda v: v.reshape(1, -1)

    hp = x_prompt.reshape(bp * s_len, d)
    hs = x_sample.reshape(bs * n_tok, d)
    tm_p = _pick_tile(s_len, 512)
    tm_s = _pick_tile(bs * n_tok, 128)
    tt = _pick_tile(s_len, SUBLANE * DELTA_CHUNK)
    outs = [[] for _ in range(8)]
    for i in range(depth):
        wi = w_in[i]
        c_dw = CONV_CH + GROUP_WIDTH
        w_in_r = jnp.concatenate(
            [wi[:, :c_dw], wi[:, c_dw + 2 * N_HEADS:], wi[:, c_dw:c_dw + 2 * N_HEADS],
             jnp.zeros((d, LANE - 2 * N_HEADS), F32)], axis=1).astype(BF16)
        alog_row = jnp.zeros((1, LANE), F32).at[0, N_HEADS:2 * N_HEADS].set(a_log[i])
        dtb_row = jnp.zeros((1, LANE), F32).at[0, N_HEADS:2 * N_HEADS].set(dt_bias[i])
        wb = lambda w: w[i].astype(BF16)
        post_w = (wb(w_out), row(ffn_norm[i]), wb(w_gate), wb(w_up), wb(w_down), wb(w_ple), wb(w_ple_gate))
        out_norm = row(final_norm)

        qkv, z, q_m, k_m, v_m, gates, k_bf, vt_bf, ksum = _in_proj(hp, row(attn_norm[i]), w_in_r, cos_p, sin_p, tm_p)
        sh = lambda a: a.reshape(bp, s_len, a.shape[-1])
        qkv3 = sh(qkv)
        hist0 = jnp.zeros((bp, CONV_WIDTH - 1, CONV_CH), F32)
        u, wk, qd, kd, qk, cd = _delta_prep(qkv3, sh(gates), hist0, conv_w[i], alog_row, dtb_row, tt)
        s00 = jnp.zeros((bp, N_HEADS, HEAD_DIM, HEAD_DIM), F32)
        o_delta, d_p = _delta_scan(u, wk, qd, kd, qk, cd, sh(z), s00, row(delta_norm[i]), tt)
        nblk = s_len // MOBA_BLOCK
        kmean = ksum.reshape(bp, s_len // tm_p, -1, GROUP_WIDTH)
        kmean = kmean.reshape(bp, -1, GROUP_WIDTH)
        if tm_p < MOBA_BLOCK:
            kmean = kmean.reshape(bp, nblk, MOBA_BLOCK // tm_p, GROUP_WIDTH).sum(axis=2)
        kmean = kmean * (1.0 / MOBA_BLOCK)
        o_moba = _moba_prompt(sh(q_m), kmean, sh(k_bf), vt_bf)
        prompt_post_in = (hp, o_delta.reshape(bp * s_len, GROUP_WIDTH), o_moba.reshape(bp * s_len, GROUP_WIDTH),
                          p_prompt[i].reshape(bp * s_len, -1))
        outs[0].append(k_m.reshape(bp, s_len, N_HEADS, HEAD_DIM))
        outs[1].append(v_m.reshape(bp, s_len, N_HEADS, HEAD_DIM))
        outs[4].append(d_p)
        outs[6].append(qkv3[:, s_len - (CONV_WIDTH - 1):])

        qkv, z, q_m, k_m, v_m, gates, _, _, _ = _in_proj(hs, row(attn_norm[i]), w_in_r, cos_s, sin_s, tm_s)
        sh = lambda a: a.reshape(bs, n_tok, a.shape[-1])
        qkv3 = sh(qkv)
        xx_full = jnp.concatenate([state_conv[i], qkv3], axis=1)
        xx = jnp.pad(xx_full, ((0, 0), (0, SUBLANE - xx_full.shape[1]), (0, 0)))
        nseq = _pick_tile(bs, 8)
        o_delta8, d_s = _delta_sample(xx, _pad_rows(sh(gates), n_tok), _pad_rows(sh(z), n_tok), state_delta[i],
                                      conv_w[i], alog_row, dtb_row, row(delta_norm[i]), n_tok, nseq)
        n_pool = cache_k.shape[1]
        page_view = (n_pool, page_size // 2, 2 * N_HEADS, HEAD_DIM)
        rows = lambda a: a.reshape(bs, n_tok * N_HEADS, HEAD_DIM)
        o_moba_s, hp = _moba_sample_and_post(pt_flat, rows(q_m), rows(k_m), rows(v_m),
                                             cache_k[i].reshape(page_view), cache_v[i].reshape(page_view), n_pages,
                                             *prompt_post_in, *post_w, out_norm)
        hs_new = _post(hs, o_delta8[:, :n_tok].reshape(bs * n_tok, GROUP_WIDTH),
                       o_moba_s.reshape(bs * n_tok, GROUP_WIDTH),
                       p_sample[i].reshape(bs * n_tok, -1), *post_w, out_norm, tm_s)
        outs[2].append(k_m.reshape(bs, n_tok, N_HEADS, HEAD_DIM))
        outs[3].append(v_m.reshape(bs, n_tok, N_HEADS, HEAD_DIM))
        outs[5].append(d_s)
        outs[7].append(xx_full[:, xx_full.shape[1] - (CONV_WIDTH - 1):])
        hs = hs_new

    y_prompt = hp.reshape(bp, s_len, d)
    y_sample = hs.reshape(bs, n_tok, d)
    st = [jnp.stack(o) for o in outs]
    return (y_prompt, y_sample, st[0], st[1], st[2], st[3], st[4], st[5], st[6], st[7])
```

```python
import functools
import math

import jax
import jax.numpy as jnp
from jax import lax
from jax.experimental import pallas as pl
from jax.experimental.pallas import tpu as pltpu

F32 = jnp.float32
BF16 = jnp.bfloat16
HI = lax.Precision.HIGHEST

HEAD_DIM = 128
N_HEADS = 4
GROUP_WIDTH = N_HEADS * HEAD_DIM
CONV_WIDTH = 4
CONV_CH = 3 * GROUP_WIDTH
DELTA_CHUNK = 64
MOBA_BLOCK = 256
MOBA_TOPK = 3
ROPE_THETA = 10000.0
RMS_EPS = 1e-6
L2_EPS = 1e-6
NEG = -1e30
LANE = 128
SUBLANE = 8
VMEM_LIMIT = 56 * 1024 * 1024

C_QKV = 0
C_Z = CONV_CH
C_QM = C_Z + GROUP_WIDTH
C_KM = C_QM + GROUP_WIDTH
C_VM = C_KM + GROUP_WIDTH
C_GATE = C_VM + GROUP_WIDTH
IN_COLS = C_GATE + LANE


def _cparams(sem):
    return pltpu.CompilerParams(dimension_semantics=sem, vmem_limit_bytes=VMEM_LIMIT)


def _resident(shape):
    nd = len(shape)
    return pl.BlockSpec(shape, lambda *_: (0,) * nd, pipeline_mode=pl.Buffered(1))


def _sigmoid(x):
    return 1.0 / (1.0 + jnp.exp(-x))


def _silu(x):
    return x * _sigmoid(x)


def _softplus(x):
    return jnp.maximum(x, 0.0) + jnp.log(1.0 + jnp.exp(-jnp.abs(x)))


def _rms(x, g):
    return x * lax.rsqrt(jnp.mean(x * x, axis=-1, keepdims=True) + RMS_EPS) * g


def _dot(a, b, prec=None):
    return jnp.dot(a, b, precision=prec, preferred_element_type=F32)


def _dot_nt(a, b, prec=None):
    return lax.dot_general(a, b, (((1,), (1,)), ((), ())), precision=prec, preferred_element_type=F32)


def _bmm(a, b, prec=None):
    return lax.dot_general(a, b, (((2,), (1,)), ((0,), (0,))), precision=prec, preferred_element_type=F32)


def _bmm_nt(a, b, prec=None):
    return lax.dot_general(a, b, (((2,), (2,)), ((0,), (0,))), precision=prec, preferred_element_type=F32)


def _bmm_tn(a, b, prec=None):
    return lax.dot_general(a, b, (((1,), (1,)), ((0,), (0,))), precision=prec, preferred_element_type=F32)


def _in_proj_kernel(x_ref, g_ref, w_ref, cos_ref, sin_ref,
                    qkv_ref, z_ref, q_ref, k_ref, v_ref, gate_ref, kbf_ref, vt_ref, ksum_ref):
    a = _rms(x_ref[...], g_ref[...]).astype(BF16)
    cos = cos_ref[...]
    sin = sin_ref[...]

    def proj(c0, width):
        return _dot(a, w_ref[:, c0:c0 + width])

    def rope(t):
        heads = []
        for h in range(N_HEADS):
            th = t[:, h * HEAD_DIM:(h + 1) * HEAD_DIM]
            heads.append(th * cos + pltpu.roll(th, HEAD_DIM // 2, axis=1) * sin)
        return jnp.concatenate(heads, axis=1)

    qkv_ref[...] = proj(C_QKV, CONV_CH)
    z_ref[...] = proj(C_Z, GROUP_WIDTH)
    q_ref[...] = rope(proj(C_QM, GROUP_WIDTH))
    k = rope(proj(C_KM, GROUP_WIDTH))
    v = proj(C_VM, GROUP_WIDTH)
    tm = k.shape[0]
    for h in range(N_HEADS):
        k_ref[pl.ds(h, tm, stride=N_HEADS), :] = k[:, h * HEAD_DIM:(h + 1) * HEAD_DIM]
        v_ref[pl.ds(h, tm, stride=N_HEADS), :] = v[:, h * HEAD_DIM:(h + 1) * HEAD_DIM]
    kbf_ref[...] = k.astype(BF16)
    vt_ref[0] = v.T.astype(BF16)
    nblk = ksum_ref.shape[1]
    ksum_ref[0] = jnp.sum(k.reshape(nblk, k.shape[0] // nblk, GROUP_WIDTH), axis=1)
    gate_ref[...] = proj(C_GATE, LANE)


def _in_proj(x, norm_w, w_in_r, cos, sin, tm):
    t, d = x.shape
    nt = t // tm
    tiles_per_seq = cos.shape[0] // tm
    n_tab = tiles_per_seq
    nblk = max(tm // MOBA_BLOCK, 1)
    tok = lambda w: pl.BlockSpec((tm, w), lambda i: (i, 0))
    tab = pl.BlockSpec((tm, HEAD_DIM), lambda i: (i % n_tab, 0))
    out_shapes = [
        jax.ShapeDtypeStruct((t, CONV_CH), F32), jax.ShapeDtypeStruct((t, GROUP_WIDTH), F32),
        jax.ShapeDtypeStruct((t, GROUP_WIDTH), F32), jax.ShapeDtypeStruct((t * N_HEADS, HEAD_DIM), F32),
        jax.ShapeDtypeStruct((t * N_HEADS, HEAD_DIM), F32), jax.ShapeDtypeStruct((t, LANE), F32),
        jax.ShapeDtypeStruct((t, GROUP_WIDTH), BF16),
        jax.ShapeDtypeStruct((nt // tiles_per_seq, GROUP_WIDTH, tiles_per_seq * tm), BF16),
        jax.ShapeDtypeStruct((nt, nblk, GROUP_WIDTH), F32),
    ]
    head_rows = pl.BlockSpec((tm * N_HEADS, HEAD_DIM), lambda i: (i, 0))
    out_specs = [tok(CONV_CH), tok(GROUP_WIDTH), tok(GROUP_WIDTH), head_rows, head_rows,
                 tok(LANE), tok(GROUP_WIDTH),
                 pl.BlockSpec((1, GROUP_WIDTH, tm), lambda i: (i // tiles_per_seq, 0, i % tiles_per_seq)),
                 pl.BlockSpec((1, nblk, GROUP_WIDTH), lambda i: (i, 0, 0))]
    return pl.pallas_call(
        _in_proj_kernel,
        grid=(nt,),
        in_specs=[tok(d), _resident((1, d)), _resident(w_in_r.shape), tab, tab],
        out_specs=out_specs,
        out_shape=out_shapes,
        compiler_params=_cparams(("arbitrary",)),
        name="in_proj",
    )(x, norm_w, w_in_r, cos, sin)


def _split(x):
    hi = x.astype(BF16)
    return hi, (x - hi.astype(F32)).astype(BF16)


def _bmm_split(a, b, nt=False):
    ah, al = _split(a)
    bh, bl = _split(b)
    lhs = jnp.concatenate([ah, al, ah], axis=2)
    if nt:
        return _bmm_nt(lhs, jnp.concatenate([bh, bh, bl], axis=2))
    return _bmm(lhs, jnp.concatenate([bh, bh, bl], axis=1))


def _dot_nt_split(a, b):
    ah, al = _split(a)
    bh, bl = _split(b)
    return _dot_nt(jnp.concatenate([ah, al, ah], axis=1), jnp.concatenate([bh, bh, bl], axis=1))


def _bmm_exact(a, b, nt=False):
    return _bmm_nt(a, b, HI) if nt else _bmm(a, b, HI)


def _chunk_prep(q, k, v, g, beta, mm):
    n, c, dh = q.shape
    row = lax.broadcasted_iota(jnp.int32, (c, c), 0)
    col = lax.broadcasted_iota(jnp.int32, (c, c), 1)
    causal = row >= col
    strict = row > col
    gx = jnp.concatenate([jnp.broadcast_to(g, (n, c, dh)),
                          jnp.where(strict, jnp.broadcast_to(g, (n, c, c)), 0.0)], axis=2)
    g1 = gx.astype(BF16)
    r1 = gx - g1.astype(F32)
    g2 = r1.astype(BF16)
    g3 = (r1 - g2.astype(F32)).astype(BF16)
    lower3 = jnp.broadcast_to(jnp.concatenate([causal.astype(BF16)] * 3, axis=1), (n, c, 3 * c))
    csum = _bmm(lower3, jnp.concatenate([g1, g2, g3], axis=1))
    gcb = csum[:, :, :dh]
    dlog = csum[:, :, dh:]
    decay = jnp.where(causal, jnp.exp(jnp.where(causal, dlog, 0.0)), 0.0)
    kb = k * beta
    kq = mm(jnp.concatenate([kb, q], axis=1), k, nt=True)
    a_mat = jnp.where(strict, kq[:, :c] * decay, 0.0)
    qk = jnp.where(causal, kq[:, c:] * decay, 0.0)
    nmat = -a_mat
    t_inv = jnp.where(row == col, 1.0, 0.0) + nmat
    n_factors = int(math.log2(c))
    if n_factors > 1:
        nmat = mm(nmat, nmat)
    for i in range(1, n_factors):
        if i + 1 < n_factors:
            both = mm(jnp.concatenate([nmat, t_inv], axis=1), nmat)
            nmat, t_inv = both[:, :c], t_inv + both[:, c:]
        else:
            t_inv = t_inv + mm(t_inv, nmat)
    egc = jnp.exp(gcb)
    uw = mm(t_inv, jnp.concatenate([v * beta, kb * egc], axis=2))
    g_last = gcb[:, c - 1:c, :]
    return uw[:, :, :dh], uw[:, :, dh:], qk, q * egc, k * jnp.exp(g_last - gcb), jnp.exp(g_last)


def _chunk_apply(s, u, wk, qk, qd, kd, cd):
    c = u.shape[1]
    xs = _bmm(jnp.concatenate([wk, qd], axis=1).astype(BF16), s.astype(BF16))
    w = u - xs[:, :c]
    wb = w.astype(BF16)
    o = xs[:, c:] + _bmm(qk.astype(BF16), wb)
    s_new = s * cd + _bmm_tn(kd.astype(BF16), wb)
    return o, s_new


def _delta_inputs(y, gates, alog, dtb, h):
    sl = lambda base: slice(base + h * HEAD_DIM, base + (h + 1) * HEAD_DIM)
    q = y[:, sl(0)]
    k = y[:, sl(GROUP_WIDTH)]
    v = y[:, sl(2 * GROUP_WIDTH)]
    q = q * lax.rsqrt(jnp.sum(q * q, axis=-1, keepdims=True) + L2_EPS) * (HEAD_DIM ** -0.5)
    k = k * lax.rsqrt(jnp.sum(k * k, axis=-1, keepdims=True) + L2_EPS)
    beta = _sigmoid(gates[:, h:h + 1])
    g = -jnp.exp(alog[:, N_HEADS + h:N_HEADS + h + 1]) * _softplus(
        gates[:, N_HEADS + h:N_HEADS + h + 1] + dtb[:, N_HEADS + h:N_HEADS + h + 1])
    return q, k, v, g, beta


def _delta_prep_kernel(qkv_ref, gate_ref, hist_ref, convw_ref, alog_ref, dtb_ref,
                       u_ref, wk_ref, qd_ref, kd_ref, qk_ref, cd_ref, prev_ref):
    tt = qkv_ref.shape[1]
    n = tt // DELTA_CHUNK

    @pl.when(pl.program_id(1) == 0)
    def _():
        prev_ref[...] = jnp.zeros_like(prev_ref)
        prev_ref[SUBLANE - (CONV_WIDTH - 1):, :] = hist_ref[0]

    x = qkv_ref[0]
    prev = prev_ref[...]
    w = convw_ref[...]
    first_rows = lax.broadcasted_iota(jnp.int32, (SUBLANE, 1), 0)
    y = x * w[CONV_WIDTH - 1:CONV_WIDTH]
    for back in range(1, CONV_WIDTH):
        shifted = pltpu.roll(x, back, axis=0)
        head = jnp.where(first_rows < back, pltpu.roll(prev, back, axis=0), shifted[:SUBLANE])
        shifted = jnp.concatenate([head, shifted[SUBLANE:]], axis=0)
        y = y + shifted * w[CONV_WIDTH - 1 - back:CONV_WIDTH - back]
    y = _silu(y)
    prev_ref[...] = x[tt - SUBLANE:]

    gates = gate_ref[0]
    alog = alog_ref[...]
    dtb = dtb_ref[...]
    for h in range(N_HEADS):
        q, k, v, g, beta = _delta_inputs(y, gates, alog, dtb, h)
        r3 = lambda t: t.reshape(n, DELTA_CHUNK, t.shape[-1])
        u, wk, qk, qd, kd, cd = _chunk_prep(r3(q), r3(k), r3(v), r3(g), r3(beta), _bmm_split)
        hs = slice(h * HEAD_DIM, (h + 1) * HEAD_DIM)
        u_ref[0, :, hs] = u.reshape(tt, HEAD_DIM)
        wk_ref[0, :, hs] = wk.reshape(tt, HEAD_DIM)
        qd_ref[0, :, hs] = qd.reshape(tt, HEAD_DIM)
        kd_ref[0, :, hs] = kd.reshape(tt, HEAD_DIM)
        qk_ref[0, :, h * DELTA_CHUNK:(h + 1) * DELTA_CHUNK] = qk.reshape(tt, DELTA_CHUNK)
        cd_ref[0, :, hs] = cd.reshape(n, HEAD_DIM)


def _delta_prep(qkv, gates, hist, conv_w, alog_row, dtb_row, tt):
    b, s, _ = qkv.shape
    n = tt // DELTA_CHUNK
    tok = lambda w: pl.BlockSpec((1, tt, w), lambda bi, j: (bi, j, 0))
    wide = jax.ShapeDtypeStruct((b, s, GROUP_WIDTH), F32)
    return pl.pallas_call(
        _delta_prep_kernel,
        grid=(b, s // tt),
        in_specs=[tok(CONV_CH), tok(LANE),
                  pl.BlockSpec((1, CONV_WIDTH - 1, CONV_CH), lambda bi, j: (bi, 0, 0)),
                  _resident(conv_w.shape), _resident((1, LANE)), _resident((1, LANE))],
        out_specs=[tok(GROUP_WIDTH), tok(GROUP_WIDTH), tok(GROUP_WIDTH), tok(GROUP_WIDTH),
                   tok(N_HEADS * DELTA_CHUNK),
                   pl.BlockSpec((1, n, GROUP_WIDTH), lambda bi, j: (bi, j, 0))],
        out_shape=[wide, wide, wide, wide,
                   jax.ShapeDtypeStruct((b, s, N_HEADS * DELTA_CHUNK), F32),
                   jax.ShapeDtypeStruct((b, s // DELTA_CHUNK, GROUP_WIDTH), F32)],
        scratch_shapes=[pltpu.VMEM((SUBLANE, CONV_CH), F32)],
        compiler_params=_cparams(("arbitrary", "arbitrary")),
        name="delta_prep",
    )(qkv, gates, hist, conv_w, alog_row, dtb_row)


def _delta_scan_kernel(u_ref, wk_ref, qd_ref, kd_ref, qk_ref, cd_ref, z_ref, s0_ref, dnorm_ref,
                       o_ref, sfin_ref, s_scr):
    nb, tt, _ = u_ref.shape
    n = tt // DELTA_CHUNK

    @pl.when(pl.program_id(0) == 0)
    def _():
        s_scr[...] = s0_ref[...]

    dnorm = dnorm_ref[...]
    for ci in range(n):
        rows = slice(ci * DELTA_CHUNK, (ci + 1) * DELTA_CHUNK)
        for b in range(nb):
            for h in range(N_HEADS):
                hs = slice(h * HEAD_DIM, (h + 1) * HEAD_DIM)
                o, s_new = _chunk_apply(
                    s_scr[b, h][None], u_ref[b, rows, hs][None], wk_ref[b, rows, hs][None],
                    qk_ref[b, rows, h * DELTA_CHUNK:(h + 1) * DELTA_CHUNK][None],
                    qd_ref[b, rows, hs][None], kd_ref[b, rows, hs][None], cd_ref[b, ci:ci + 1, hs][None])
                s_scr[b, h] = s_new[0]
                o_ref[b, rows, hs] = _rms(o[0], dnorm) * _silu(z_ref[b, rows, hs])

    @pl.when(pl.program_id(0) == pl.num_programs(0) - 1)
    def _():
        sfin_ref[...] = s_scr[...]


def _delta_scan(u, wk, qd, kd, qk, cd, z, s0, dnorm, tt):
    b, s, _ = u.shape
    n = tt // DELTA_CHUNK
    tok = lambda w: pl.BlockSpec((b, tt, w), lambda j: (0, j, 0))
    st = pl.BlockSpec(s0.shape, lambda j: (0, 0, 0, 0))
    return pl.pallas_call(
        _delta_scan_kernel,
        grid=(s // tt,),
        in_specs=[tok(GROUP_WIDTH), tok(GROUP_WIDTH), tok(GROUP_WIDTH), tok(GROUP_WIDTH),
                  tok(N_HEADS * DELTA_CHUNK), pl.BlockSpec((b, n, GROUP_WIDTH), lambda j: (0, j, 0)),
                  tok(GROUP_WIDTH), st, _resident((1, HEAD_DIM))],
        out_specs=[tok(GROUP_WIDTH), st],
        out_shape=[jax.ShapeDtypeStruct((b, s, GROUP_WIDTH), F32), jax.ShapeDtypeStruct(s0.shape, F32)],
        scratch_shapes=[pltpu.VMEM(s0.shape, F32)],
        compiler_params=_cparams(("arbitrary",)),
        name="delta_scan",
    )(u, wk, qd, kd, qk, cd, z, s0, dnorm)


def _topk_blocks(gate, valid, axis):
    nb = gate.shape[axis]
    blk = lax.broadcasted_iota(jnp.int32, gate.shape, axis).astype(F32)
    gv = jnp.where(valid, gate, -jnp.inf)
    hits = []
    for _ in range(MOBA_TOPK):
        top = jnp.max(gv, axis=axis, keepdims=True)
        first = jnp.min(jnp.where(gv == top, blk, float(nb)), axis=axis, keepdims=True)
        hit = (blk == first) & (top > -jnp.inf)
        gv = jnp.where(hit, -jnp.inf, gv)
        hits.append(hit)
    return hits


def _moba_prompt_kernel(q_ref, kmean_ref, k_ref, vt_ref, o_ref,
                        bias_scr, qbf_scr, m_scr, l_scr, acc_scr, sa_scr, sb_scr):
    cur = pl.program_id(1)
    tq = q_ref.shape[1]
    nb = kmean_ref.shape[1]
    scale = HEAD_DIM ** -0.5
    key_i = lax.broadcasted_iota(jnp.int32, (MOBA_BLOCK, tq), 0)
    qry_i = lax.broadcasted_iota(jnp.int32, (MOBA_BLOCK, tq), 1)
    causal_bias = jnp.where(key_i <= qry_i, 0.0, NEG)
    blk_ids = lax.broadcasted_iota(jnp.int32, (nb, tq), 0)
    heads = [slice(h * HEAD_DIM, (h + 1) * HEAD_DIM) for h in range(N_HEADS)]

    for h, hs in enumerate(heads):
        q = q_ref[0, :, hs]
        gate_t = _dot_nt_split(kmean_ref[0, :, hs], q)
        chosen = jnp.zeros((nb, tq), jnp.bool_)
        for hit in _topk_blocks(gate_t, blk_ids < cur, 0):
            chosen = chosen | hit
        bias_scr[h] = jnp.where(chosen, 0.0, NEG)
        qbf_scr[h] = (q * scale).astype(BF16)
        m_scr[h] = jnp.full((1, tq), NEG, F32)
        l_scr[h] = jnp.zeros((1, tq), F32)
        acc_scr[h] = jnp.zeros((HEAD_DIM, tq), F32)

    def scores(jb, s_buf):
        r0 = pl.multiple_of(jb * MOBA_BLOCK, MOBA_BLOCK)
        for h, hs in enumerate(heads):
            s_buf[h] = _dot_nt(k_ref[0, pl.ds(r0, MOBA_BLOCK), hs], qbf_scr[h])

    def update(jb, s_buf, bias_of):
        r0 = pl.multiple_of(jb * MOBA_BLOCK, MOBA_BLOCK)
        for h, hs in enumerate(heads):
            s = s_buf[h] + bias_of(h)
            m_i = m_scr[h]
            m_new = jnp.maximum(m_i, jnp.max(s, axis=0, keepdims=True))
            alpha = jnp.exp(m_i - m_new)
            p = jnp.exp(s - m_new)
            m_scr[h] = m_new
            l_scr[h] = alpha * l_scr[h] + jnp.sum(p, axis=0, keepdims=True)
            acc_scr[h] = alpha * acc_scr[h] + _dot(vt_ref[0, hs, pl.ds(r0, MOBA_BLOCK)], p.astype(BF16))

    def chosen_bias(jb):
        return lambda h: bias_scr[h, pl.ds(jb, 1), :]

    def own_bias(h):
        return causal_bias

    scores(0, sa_scr)

    def block_pair(pair, carry):
        jb = 2 * pair
        scores(jb + 1, sb_scr)
        update(jb, sa_scr, chosen_bias(jb))
        scores(jb + 2, sa_scr)
        update(jb + 1, sb_scr, chosen_bias(jb + 1))
        return carry

    lax.fori_loop(0, cur // 2, block_pair, 0)

    @pl.when(cur % 2 == 1)
    def _():
        scores(cur, sb_scr)
        update(cur - 1, sa_scr, chosen_bias(cur - 1))
        update(cur, sb_scr, own_bias)

    @pl.when(cur % 2 == 0)
    def _():
        update(cur, sa_scr, own_bias)

    for h, hs in enumerate(heads):
        o_ref[0, :, hs] = (acc_scr[h] / l_scr[h]).T


def _moba_prompt(q, kmean, k_bf, vt_bf):
    b, s, _ = q.shape
    nb = s // MOBA_BLOCK
    tok = pl.BlockSpec((1, MOBA_BLOCK, GROUP_WIDTH), lambda bi, j: (bi, j, 0))
    seq = lambda a: pl.BlockSpec((1,) + a.shape[1:], lambda bi, j: (bi, 0, 0))
    return pl.pallas_call(
        _moba_prompt_kernel,
        grid=(b, nb),
        in_specs=[tok, seq(kmean), seq(k_bf), seq(vt_bf)],
        out_specs=tok,
        out_shape=jax.ShapeDtypeStruct((b, s, GROUP_WIDTH), F32),
        scratch_shapes=[pltpu.VMEM((N_HEADS, nb, MOBA_BLOCK), F32),
                        pltpu.VMEM((N_HEADS, MOBA_BLOCK, HEAD_DIM), BF16),
                        pltpu.VMEM((N_HEADS, 1, MOBA_BLOCK), F32),
                        pltpu.VMEM((N_HEADS, 1, MOBA_BLOCK), F32),
                        pltpu.VMEM((N_HEADS, HEAD_DIM, MOBA_BLOCK), F32),
                        pltpu.VMEM((N_HEADS, MOBA_BLOCK, MOBA_BLOCK), F32),
                        pltpu.VMEM((N_HEADS, MOBA_BLOCK, MOBA_BLOCK), F32)],
        compiler_params=_cparams(("arbitrary", "arbitrary")),
        name="moba_prompt",
    )(q, kmean, k_bf, vt_bf)


FF_CHUNK = 256
N_POST_STAGES = 8


def _post_stages(x_ref, od_ref, om_ref, ple_ref, wout_ref, fnorm_ref, wg_ref, wu_ref, wd_ref,
                 wple_ref, wpg_ref, onorm_ref, y_ref, h_scr, f_scr, ffn_scr):
    d_ff = wg_ref.shape[1]
    ff = FF_CHUNK if d_ff % FF_CHUNK == 0 else d_ff
    starts = list(range(0, d_ff, ff))
    n_mid = N_POST_STAGES - 2
    groups = [starts[i * len(starts) // n_mid:(i + 1) * len(starts) // n_mid] for i in range(n_mid)]

    def head():
        mix = jnp.concatenate([od_ref[...], om_ref[...]], axis=1).astype(BF16)
        h = x_ref[...] + _dot(mix, wout_ref[...])
        h_scr[...] = h
        f_scr[...] = _rms(h, fnorm_ref[...]).astype(BF16)
        ffn_scr[...] = jnp.zeros_like(ffn_scr)

    def ffn(group):
        def run():
            f = f_scr[...]
            part = None
            for c0 in group:
                gate = _dot(f, wg_ref[:, c0:c0 + ff])
                up = _dot(f, wu_ref[:, c0:c0 + ff])
                down = _dot((_silu(gate) * up).astype(BF16), wd_ref[c0:c0 + ff, :])
                part = down if part is None else part + down
            if part is not None:
                ffn_scr[...] += part
        return run

    def tail():
        h = h_scr[...] + ffn_scr[...]
        h = h + _dot(ple_ref[...].astype(BF16), wple_ref[...]) * _sigmoid(_dot(h.astype(BF16), wpg_ref[...]))
        y_ref[...] = _rms(h, onorm_ref[...])

    return [head] + [ffn(g) for g in groups] + [tail]


def _post_scratch(tm, d):
    return [pltpu.VMEM((tm, d), F32), pltpu.VMEM((tm, d), BF16), pltpu.VMEM((tm, d), F32)]


def _post_kernel(*refs):
    for stage in _post_stages(*refs):
        stage()


def _post(x, o_delta, o_moba, ple, w_out, ffn_norm, w_gate, w_up, w_down, w_ple, w_ple_gate, out_norm, tm):
    t, d = x.shape
    tok = lambda w: pl.BlockSpec((tm, w), lambda i: (i, 0))
    return pl.pallas_call(
        _post_kernel,
        grid=(t // tm,),
        in_specs=[tok(d), tok(GROUP_WIDTH), tok(GROUP_WIDTH), tok(ple.shape[1]),
                  _resident(w_out.shape), _resident((1, d)), _resident(w_gate.shape), _resident(w_up.shape),
                  _resident(w_down.shape), _resident(w_ple.shape), _resident(w_ple_gate.shape), _resident((1, d))],
        out_specs=tok(d),
        out_shape=jax.ShapeDtypeStruct((t, d), F32),
        scratch_shapes=_post_scratch(tm, d),
        compiler_params=_cparams(("arbitrary",)),
        name="post",
    )(x, o_delta, o_moba, ple, w_out, ffn_norm, w_gate, w_up, w_down, w_ple, w_ple_gate, out_norm)


def _delta_sample_kernel(xx_ref, gate_ref, z_ref, s_ref, convw_ref, alog_ref, dtb_ref, dnorm_ref,
                         o_ref, snew_ref, *, n_tok):
    nseq = xx_ref.shape[0]
    rows = nseq * SUBLANE
    xx = xx_ref[...].reshape(rows, CONV_CH)
    w = convw_ref[...]
    y = xx * w[0:1]
    for j in range(1, CONV_WIDTH):
        y = y + pltpu.roll(xx, rows - j, axis=0) * w[j:j + 1]
    y = _silu(y)
    gates = gate_ref[...].reshape(rows, LANE)
    real = (lax.broadcasted_iota(jnp.int32, (rows, 1), 0) % SUBLANE) < n_tok
    dnorm = dnorm_ref[...]
    for h in range(N_HEADS):
        q, k, v, g, beta = _delta_inputs(y, gates, alog_ref[...], dtb_ref[...], h)
        g = jnp.where(real, g, 0.0)
        beta = jnp.where(real, beta, 0.0)
        r3 = lambda t: t.reshape(nseq, SUBLANE, t.shape[-1])
        o, s_new = _chunk_apply(s_ref[:, h], *_chunk_prep(r3(q), r3(k), r3(v), r3(g), r3(beta), _bmm_exact))
        snew_ref[:, h] = s_new
        hs = slice(h * HEAD_DIM, (h + 1) * HEAD_DIM)
        o_ref[:, :, hs] = _rms(o, dnorm) * _silu(z_ref[:, :, hs])


def _delta_sample(xx, gates8, z8, state, conv_w, alog_row, dtb_row, dnorm, n_tok, nseq):
    nb = xx.shape[0]
    seq = lambda w: pl.BlockSpec((nseq, SUBLANE, w), lambda i: (i, 0, 0))
    st = pl.BlockSpec((nseq,) + state.shape[1:], lambda i: (i, 0, 0, 0))
    return pl.pallas_call(
        functools.partial(_delta_sample_kernel, n_tok=n_tok),
        grid=(nb // nseq,),
        in_specs=[seq(CONV_CH), seq(LANE), seq(GROUP_WIDTH), st, _resident(conv_w.shape),
                  _resident((1, LANE)), _resident((1, LANE)), _resident((1, HEAD_DIM))],
        out_specs=[seq(GROUP_WIDTH), st],
        out_shape=[jax.ShapeDtypeStruct((nb, SUBLANE, GROUP_WIDTH), F32), jax.ShapeDtypeStruct(state.shape, F32)],
        compiler_params=_cparams(("arbitrary",)),
        name="delta_sample",
    )(xx, gates8, z8, state, conv_w, alog_row, dtb_row, dnorm)


PAGES_PER_CHUNK = 16
N_SLOTS = 4


def _moba_sample_kernel(pt_ref, q_ref, kn_ref, vn_ref, kc_ref, vc_ref, *rest, n_pages, page_size):
    post_in, (o_ref, y_ref), (buf, sem, s_scr, p_scr, psum, h_scr, f_scr, ffn_scr) = rest[:12], rest[12:14], rest[14:]
    stages = _post_stages(*post_in, y_ref, h_scr, f_scr, ffn_scr)
    b = pl.program_id(0)
    nseq = pl.num_programs(0)
    n_ck = n_pages // PAGES_PER_CHUNK
    n_stream = 2 * n_ck
    rows_per_page = page_size * N_HEADS
    cols = PAGES_PER_CHUNK * rows_per_page
    blk_cols = MOBA_BLOCK * N_HEADS
    n_blocks = n_pages * page_size // MOBA_BLOCK
    nq = q_ref.shape[1]
    scale = HEAD_DIM ** -0.5

    def copies(seq, j, slot):
        src = kc_ref if j < n_ck else vc_ref
        first = seq * n_pages + (j % n_ck) * PAGES_PER_CHUNK
        return [pltpu.make_async_copy(src.at[pt_ref[first + i]], buf.at[slot, i], sem.at[slot])
                for i in range(PAGES_PER_CHUNK)]

    def start_ahead(j):
        ahead = j + N_SLOTS - 1
        if ahead < n_stream:
            for cp in copies(b, ahead, ahead % N_SLOTS):
                cp.start()
        else:
            @pl.when(b + 1 < nseq)
            def _():
                for cp in copies(b + 1, ahead - n_stream, ahead % N_SLOTS):
                    cp.start()

    @pl.when(b == 0)
    def _():
        for j in range(N_SLOTS - 1):
            for cp in copies(0, j, j):
                cp.start()

    def run_stages(j):
        for stage in stages[j * len(stages) // n_stream:(j + 1) * len(stages) // n_stream]:
            stage()

    q = q_ref[0]
    q_bf = q.astype(BF16)
    head_ok = (lax.broadcasted_iota(jnp.int32, (nq, blk_cols), 0) % N_HEADS
               == lax.broadcasted_iota(jnp.int32, (nq, blk_cols), 1) % N_HEADS)
    blk_lane = lax.broadcasted_iota(jnp.int32, (nq, n_blocks), 1)
    blocks_per_chunk = cols // blk_cols

    blk_max = jnp.full((nq, n_blocks), NEG, F32)
    for j in range(n_ck):
        slot = j % N_SLOTS
        for cp in copies(b, j, slot):
            cp.wait()
        start_ahead(j)
        pages = buf[slot]
        psum[j * PAGES_PER_CHUNK:(j + 1) * PAGES_PER_CHUNK] = jnp.sum(pages, axis=1)
        s_chunk = _dot_nt(q_bf, pages.reshape(cols, HEAD_DIM).astype(BF16)) * scale
        s_scr[:, j * cols:(j + 1) * cols] = s_chunk
        for i in range(blocks_per_chunk):
            top = jnp.max(jnp.where(head_ok, s_chunk[:, i * blk_cols:(i + 1) * blk_cols], NEG), axis=1, keepdims=True)
            blk_max = jnp.where(blk_lane == j * blocks_per_chunk + i, top, blk_max)
        run_stages(j)

    prow = n_pages * 2 * N_HEADS
    gate_rows = _dot_nt(q, psum[...].reshape(prow, HEAD_DIM), HI)
    r_head = lax.broadcasted_iota(jnp.int32, (nq, prow), 0) % N_HEADS
    c_head = lax.broadcasted_iota(jnp.int32, (nq, prow), 1) % N_HEADS
    gate_rows = jnp.where(r_head == c_head, gate_rows, 0.0)
    rows_per_block = prow // n_blocks
    fold = jnp.where(lax.broadcasted_iota(jnp.int32, (prow, n_blocks), 0) // rows_per_block
                     == lax.broadcasted_iota(jnp.int32, (prow, n_blocks), 1), 1.0, 0.0)
    gate = _dot(gate_rows, fold, HI) * (1.0 / MOBA_BLOCK)
    sel = jnp.zeros((nq, n_blocks), F32)
    for hit in _topk_blocks(gate, jnp.full((nq, n_blocks), True), 1):
        sel = jnp.where(hit, 1.0, sel)

    orow = lax.broadcasted_iota(jnp.int32, (nq, nq), 0)
    ocol = lax.broadcasted_iota(jnp.int32, (nq, nq), 1)
    own_ok = (orow % N_HEADS == ocol % N_HEADS) & (ocol // N_HEADS <= orow // N_HEADS)
    s_own = jnp.where(own_ok, _dot_nt(q_bf, kn_ref[0].astype(BF16)) * scale, NEG)
    m = jnp.maximum(jnp.max(s_own, axis=1, keepdims=True),
                    jnp.max(jnp.where(sel > 0.5, blk_max, NEG), axis=1, keepdims=True))
    p_own = jnp.exp(s_own - m)
    denom = jnp.sum(p_own, axis=1, keepdims=True)
    for n in range(n_blocks):
        cs = slice(n * blk_cols, (n + 1) * blk_cols)
        p_blk = jnp.where(head_ok & (sel[:, n:n + 1] > 0.5), jnp.exp(s_scr[:, cs] - m), 0.0)
        denom = denom + jnp.sum(p_blk, axis=1, keepdims=True)
        p_scr[:, cs] = p_blk.astype(BF16)

    acc = _dot(p_own.astype(BF16), vn_ref[0].astype(BF16))
    for j in range(n_ck, n_stream):
        slot = j % N_SLOTS
        for cp in copies(b, j, slot):
            cp.wait()
        start_ahead(j)
        jc = j - n_ck
        acc = acc + _dot(p_scr[:, jc * cols:(jc + 1) * cols], buf[slot].reshape(cols, HEAD_DIM).astype(BF16))
        run_stages(j)
    o_ref[0] = acc / denom


def _moba_sample_and_post(pt_flat, q16, kn16, vn16, cache_k4, cache_v4, n_pages,
                          x, o_delta, o_moba, ple, w_out, ffn_norm, w_gate, w_up, w_down, w_ple, w_ple_gate, out_norm):
    nb, nq, _ = q16.shape
    t, d = x.shape
    tm = t // nb
    assert t % nb == 0 and tm % SUBLANE == 0
    page_size = cache_k4.shape[1] * cache_k4.shape[2] // N_HEADS
    n_cols = n_pages * page_size * N_HEADS
    seq = pl.BlockSpec((1, nq, HEAD_DIM), lambda i, pt: (i, 0, 0))
    tok = lambda w: pl.BlockSpec((tm, w), lambda i, pt: (i, 0))
    weights = (w_out, ffn_norm, w_gate, w_up, w_down, w_ple, w_ple_gate, out_norm)
    return pl.pallas_call(
        functools.partial(_moba_sample_kernel, n_pages=n_pages, page_size=page_size),
        grid_spec=pltpu.PrefetchScalarGridSpec(
            num_scalar_prefetch=1,
            grid=(nb,),
            in_specs=[seq, seq, seq, pl.BlockSpec(memory_space=pl.ANY), pl.BlockSpec(memory_space=pl.ANY),
                      tok(d), tok(GROUP_WIDTH), tok(GROUP_WIDTH), tok(ple.shape[1])]
                     + [_resident(w.shape) for w in weights],
            out_specs=[seq, tok(d)],
            scratch_shapes=[pltpu.VMEM((N_SLOTS, PAGES_PER_CHUNK) + cache_k4.shape[1:], F32),
                            pltpu.SemaphoreType.DMA((N_SLOTS,)),
                            pltpu.VMEM((nq, n_cols), F32),
                            pltpu.VMEM((nq, n_cols), BF16),
                            pltpu.VMEM((n_pages, 2 * N_HEADS, HEAD_DIM), F32)] + _post_scratch(tm, d)),
        out_shape=[jax.ShapeDtypeStruct((nb, nq, HEAD_DIM), F32), jax.ShapeDtypeStruct((t, d), F32)],
        compiler_params=_cparams(("arbitrary",)),
        name="moba_sample_post",
    )(pt_flat, q16, kn16, vn16, cache_k4, cache_v4, x, o_delta, o_moba, ple, *weights)


def _rope_tables(pos):
    half = HEAD_DIM // 2
    inv_freq = jnp.exp(jnp.arange(half, dtype=F32) * (-2.0 * math.log(ROPE_THETA) / HEAD_DIM))
    ang = pos.astype(F32)[:, None] * inv_freq[None, :]
    cos, sin = jnp.cos(ang), jnp.sin(ang)
    return jnp.concatenate([cos, cos], axis=-1), jnp.concatenate([-sin, sin], axis=-1)


def _pad_rows(a, n_tok):
    return jnp.pad(a, ((0, 0), (0, SUBLANE - n_tok), (0, 0)))


def _pick_tile(n, pref):
    t = min(pref, n)
    while n % t:
        t //= 2
    return t


def kernel(x_prompt, x_sample, cache_k, cache_v, page_table, state_delta, state_conv, p_prompt, p_sample,
           attn_norm, w_in, conv_w, a_log, dt_bias, delta_norm, w_out, ffn_norm, w_gate, w_up, w_down,
           w_ple, w_ple_gate, final_norm):
    bp, s_len, d = x_prompt.shape
    bs, n_tok, _ = x_sample.shape
    depth = w_in.shape[0]
    n_pages = page_table.shape[1]
    page_size = cache_k.shape[2]
    assert depth == 1, "intermediate layers would need the un-normalised hidden state"
    assert n_tok <= SUBLANE - (CONV_WIDTH - 1) and s_len % MOBA_BLOCK == 0 and s_len >= SUBLANE
    assert (n_pages * page_size) % MOBA_BLOCK == 0 and n_pages % PAGES_PER_CHUNK == 0 and page_size % 2 == 0
    assert (2 * n_pages // PAGES_PER_CHUNK) % N_SLOTS == 0
    assert n_pages * page_size // MOBA_BLOCK >= MOBA_TOPK and s_len // MOBA_BLOCK > MOBA_TOPK

    past = n_pages * page_size
    cos_p, sin_p = _rope_tables(jnp.arange(s_len, dtype=jnp.int32))
    cos_s, sin_s = _rope_tables(past + jnp.arange(n_tok, dtype=jnp.int32))
    cos_s, sin_s = jnp.tile(cos_s, (bs, 1)), jnp.tile(sin_s, (bs, 1))
    pt_flat = page_table.reshape(-1)
    row = lambda v: v.reshape(1, -1)

    hp = x_prompt.reshape(bp * s_len, d)
    hs = x_sample.reshape(bs * n_tok, d)
    tm_p = _pick_tile(s_len, 512)
    tm_s = _pick_tile(bs * n_tok, 128)
    tt = _pick_tile(s_len, SUBLANE * DELTA_CHUNK)
    outs = [[] for _ in range(8)]
    for i in range(depth):
        wi = w_in[i]
        c_dw = CONV_CH + GROUP_WIDTH
        w_in_r = jnp.concatenate(
            [wi[:, :c_dw], wi[:, c_dw + 2 * N_HEADS:], wi[:, c_dw:c_dw + 2 * N_HEADS],
             jnp.zeros((d, LANE - 2 * N_HEADS), F32)], axis=1).astype(BF16)
        alog_row = jnp.zeros((1, LANE), F32).at[0, N_HEADS:2 * N_HEADS].set(a_log[i])
        dtb_row = jnp.zeros((1, LANE), F32).at[0, N_HEADS:2 * N_HEADS].set(dt_bias[i])
        wb = lambda w: w[i].astype(BF16)
        post_w = (wb(w_out), row(ffn_norm[i]), wb(w_gate), wb(w_up), wb(w_down), wb(w_ple), wb(w_ple_gate))
        out_norm = row(final_norm)

        qkv, z, q_m, k_m, v_m, gates, k_bf, vt_bf, ksum = _in_proj(hp, row(attn_norm[i]), w_in_r, cos_p, sin_p, tm_p)
        sh = lambda a: a.reshape(bp, s_len, a.shape[-1])
        qkv3 = sh(qkv)
        hist0 = jnp.zeros((bp, CONV_WIDTH - 1, CONV_CH), F32)
        tt_prep = _pick_tile(s_len, 2 * tt)
        u, wk, qd, kd, qk, cd = _delta_prep(qkv3, sh(gates), hist0, conv_w[i], alog_row, dtb_row, tt_prep)
        s00 = jnp.zeros((bp, N_HEADS, HEAD_DIM, HEAD_DIM), F32)
        o_delta, d_p = _delta_scan(u, wk, qd, kd, qk, cd, sh(z), s00, row(delta_norm[i]), tt)
        nblk = s_len // MOBA_BLOCK
        kmean = ksum.reshape(bp, s_len // tm_p, -1, GROUP_WIDTH)
        kmean = kmean.reshape(bp, -1, GROUP_WIDTH)
        if tm_p < MOBA_BLOCK:
            kmean = kmean.reshape(bp, nblk, MOBA_BLOCK // tm_p, GROUP_WIDTH).sum(axis=2)
        kmean = kmean * (1.0 / MOBA_BLOCK)
        o_moba = _moba_prompt(sh(q_m), kmean, sh(k_bf), vt_bf)
        prompt_post_in = (hp, o_delta.reshape(bp * s_len, GROUP_WIDTH), o_moba.reshape(bp * s_len, GROUP_WIDTH),
                          p_prompt[i].reshape(bp * s_len, -1))
        outs[0].append(k_m.reshape(bp, s_len, N_HEADS, HEAD_DIM))
        outs[1].append(v_m.reshape(bp, s_len, N_HEADS, HEAD_DIM))
        outs[4].append(d_p)
        outs[6].append(qkv3[:, s_len - (CONV_WIDTH - 1):])

        qkv, z, q_m, k_m, v_m, gates, _, _, _ = _in_proj(hs, row(attn_norm[i]), w_in_r, cos_s, sin_s, tm_s)
        sh = lambda a: a.reshape(bs, n_tok, a.shape[-1])
        qkv3 = sh(qkv)
        xx_full = jnp.concatenate([state_conv[i], qkv3], axis=1)
        xx = jnp.pad(xx_full, ((0, 0), (0, SUBLANE - xx_full.shape[1]), (0, 0)))
        nseq = _pick_tile(bs, 8)
        o_delta8, d_s = _delta_sample(xx, _pad_rows(sh(gates), n_tok), _pad_rows(sh(z), n_tok), state_delta[i],
                                      conv_w[i], alog_row, dtb_row, row(delta_norm[i]), n_tok, nseq)
        n_pool = cache_k.shape[1]
        page_view = (n_pool, page_size // 2, 2 * N_HEADS, HEAD_DIM)
        rows = lambda a: a.reshape(bs, n_tok * N_HEADS, HEAD_DIM)
        o_moba_s, hp = _moba_sample_and_post(pt_flat, rows(q_m), rows(k_m), rows(v_m),
                                             cache_k[i].reshape(page_view), cache_v[i].reshape(page_view), n_pages,
                                             *prompt_post_in, *post_w, out_norm)
        hs_new = _post(hs, o_delta8[:, :n_tok].reshape(bs * n_tok, GROUP_WIDTH),
                       o_moba_s.reshape(bs * n_tok, GROUP_WIDTH),
                       p_sample[i].reshape(bs * n_tok, -1), *post_w, out_norm, tm_s)
        outs[2].append(k_m.reshape(bs, n_tok, N_HEADS, HEAD_DIM))
        outs[3].append(v_m.reshape(bs, n_tok, N_HEADS, HEAD_DIM))
        outs[5].append(d_s)
        outs[7].append(xx_full[:, xx_full.shape[1] - (CONV_WIDTH - 1):])
        hs = hs_new

    y_prompt = hp.reshape(bp, s_len, d)
    y_sample = hs.reshape(bs, n_tok, d)
    st = [jnp.stack(o) for o in outs]
    return (y_prompt, y_sample, st[0], st[1], st[2], st[3], st[4], st[5], st[6], st[7])
```
